```python
import math
import jax, jax.numpy as jnp
from jax import lax
import numpy as np

D_MODEL = 2048
BATCH = 4
SEQ = 4096
DEPTH = 1

CHUNK = 64
Q_BLOCK = 128

ML_HEADS = 4
ML_DQK = 128
ML_DV = 256
ML_CONV = 4
GATE_CAP = 15.0

DA_HEADS = 8
DA_DH = 64
DA_DV = 2 * DA_DH
ROPE_DIM = DA_DH // 4
ROPE_THETA = 500000.0

ML_WIDTH = ML_HEADS * ML_DV
DA_WIDTH = DA_HEADS * DA_DV
MIX_WIDTH = ML_WIDTH + DA_WIDTH

IN_SPLITS = (ML_HEADS * ML_DQK, ML_HEADS * ML_DQK, ML_WIDTH, ML_WIDTH, ML_HEADS, ML_HEADS,
             DA_HEADS * 2 * DA_DH, DA_HEADS * 2 * DA_DH, DA_WIDTH)
N_IN = (4 * ML_HEADS * ML_DQK // 2) + 2 * ML_WIDTH + 2 * ML_HEADS + 4 * DA_HEADS * DA_DH + DA_WIDTH

D_FF = ((8 * D_MODEL + 3 * 256 - 1) // (3 * 256)) * 256

kernel_name = "hybrid_mlstm_diffattn_chunk_causal_block"


def rms_norm(x, w, eps=1e-6):
    xf = x.astype(jnp.float32)
    y = xf * lax.rsqrt(jnp.mean(xf * xf, axis=-1, keepdims=True) + eps)
    return (y * w.astype(jnp.float32)).astype(x.dtype)


def causal_dwconv(x, w, b):
    k = w.shape[0]
    y = lax.conv_general_dilated(x, w[:, None, :].astype(x.dtype), window_strides=(1,),
                                 padding=[(k - 1, 0)], dimension_numbers=('NWC', 'WIO', 'NWC'),
                                 feature_group_count=x.shape[-1])
    return y + b.astype(x.dtype)


def partial_rope(x, positions):
    half = ROPE_DIM // 2
    inv_freq = ROPE_THETA ** (-jnp.arange(0, ROPE_DIM, 2, dtype=jnp.float32) / ROPE_DIM)
    ang = positions.astype(jnp.float32)[..., None] * inv_freq
    cos = jnp.cos(ang)[:, :, None, None, :]
    sin = jnp.sin(ang)[:, :, None, None, :]
    xf = x.astype(jnp.float32)
    x1 = xf[..., :half]
    x2 = xf[..., half:ROPE_DIM]
    rot = jnp.concatenate([x1 * cos - x2 * sin, x2 * cos + x1 * sin, xf[..., ROPE_DIM:]], axis=-1)
    return rot.astype(x.dtype)


def mlstm_chunkwise(q, k, v, li, lf):
    B, H, S, _ = q.shape
    nc = S // CHUNK
    f32 = jnp.float32
    q = q.astype(f32).reshape(B, H, nc, CHUNK, ML_DQK) * (ML_DQK ** -0.5)
    k = k.astype(f32).reshape(B, H, nc, CHUNK, ML_DQK)
    v = v.astype(f32).reshape(B, H, nc, CHUNK, ML_DV)
    li = li.reshape(B, H, nc, CHUNK)
    lf = lf.reshape(B, H, nc, CHUNK)
    b = jnp.cumsum(lf, axis=-1)
    b_end = b[..., -1]
    a = b_end[..., None] - b + li
    m_loc = jnp.max(a, axis=-1)
    w = jnp.exp(a - m_loc[..., None])
    C_loc = jnp.einsum('bhclv,bhclk->bhcvk', v * w[..., None], k)
    n_loc = jnp.einsum('bhcl,bhclk->bhck', w, k)

    def step(carry, inp):
        C, n, m = carry
        Cl, nl, ml, be = inp
        m_new = jnp.maximum(be + m, ml)
        f_old = jnp.exp(be + m - m_new)
        f_loc = jnp.exp(ml - m_new)
        C_new = f_old[..., None, None] * C + f_loc[..., None, None] * Cl
        n_new = f_old[..., None] * n + f_loc[..., None] * nl
        return (C_new, n_new, m_new), (C, n, m)

    init = (jnp.zeros((B, H, ML_DV, ML_DQK), f32), jnp.zeros((B, H, ML_DQK), f32), jnp.zeros((B, H), f32))
    xs = (jnp.moveaxis(C_loc, 2, 0), jnp.moveaxis(n_loc, 2, 0), jnp.moveaxis(m_loc, 2, 0), jnp.moveaxis(b_end, 2, 0))
    _, (C_prev, n_prev, m_prev) = lax.scan(step, init, xs)
    C_prev = jnp.moveaxis(C_prev, 0, 2)
    n_prev = jnp.moveaxis(n_prev, 0, 2)
    m_prev = jnp.moveaxis(m_prev, 0, 2)

    causal = jnp.tril(jnp.ones((CHUNK, CHUNK), dtype=bool))
    log_d = jnp.where(causal, b[..., :, None] - b[..., None, :] + li[..., None, :], -jnp.inf)
    g = b + m_prev[..., None]
    m_t = jnp.maximum(g, jnp.max(log_d, axis=-1))
    d = jnp.exp(log_d - m_t[..., None])
    inter = jnp.exp(g - m_t)
    qk = jnp.einsum('bhctd,bhcsd->bhcts', q, k) * d
    num = inter[..., None] * jnp.einsum('bhcvk,bhctk->bhctv', C_prev, q) + jnp.einsum('bhcts,bhcsv->bhctv', qk, v)
    den = inter * jnp.einsum('bhck,bhctk->bhct', n_prev, q) + jnp.sum(qk, axis=-1)
    h = num / jnp.maximum(jnp.abs(den), jnp.exp(-m_t))[..., None]
    return h.reshape(B, H, S, ML_DV)


def diff_attention(q, k, v, lam):
    S = q.shape[3]
    scale = DA_DH ** -0.5
    outs = []
    for j in range(S // Q_BLOCK):
        q0 = j * Q_BLOCK
        kend = q0 + Q_BLOCK
        qb = q[:, :, :, q0:kend]
        kb = k[:, :, :, :kend]
        vb = v[:, :, :kend]
        s = jnp.einsum('bhmqd,bhmkd->bhmqk', qb, kb).astype(jnp.float32) * scale
        q_chunk = (q0 + jnp.arange(Q_BLOCK)) // CHUNK
        k_chunk = jnp.arange(kend) // CHUNK
        allowed = k_chunk[None, :] <= q_chunk[:, None]
        p = jax.nn.softmax(jnp.where(allowed, s, -jnp.inf), axis=-1)
        a = p[:, :, 0] - lam * p[:, :, 1]
        outs.append(jnp.einsum('bhqk,bhkd->bhqd', a.astype(v.dtype), vb))
    return jnp.concatenate(outs, axis=2)


def setup_inputs(seed: int = 0) -> dict:
    key = jax.random.key(seed)
    ks = jax.random.split(key, 24)
    f32 = jnp.float32
    nrm = lambda k, shape, s: jax.random.normal(k, shape, f32) * s
    x = jax.random.normal(ks[0], (BATCH, SEQ, D_MODEL), f32)
    c = jax.random.normal(ks[1], (BATCH, D_MODEL), f32)
    offset = jax.random.randint(ks[2], (BATCH, 1), 0, 4096, dtype=jnp.int32)
    positions = offset + jnp.arange(SEQ, dtype=jnp.int32)[None, :]
    gate_b = jnp.concatenate([nrm(ks[10], (DEPTH, ML_HEADS), 0.1),
                              3.0 + nrm(ks[11], (DEPTH, ML_HEADS), 0.5)], axis=-1)
    return {
        "x": x,
        "c": c,
        "positions": positions,
        "norm1_w": 1.0 + nrm(ks[3], (DEPTH, D_MODEL), 0.02),
        "norm2_w": 1.0 + nrm(ks[4], (DEPTH, D_MODEL), 0.02),
        "w_ada": nrm(ks[5], (DEPTH, D_MODEL, 6 * D_MODEL), 0.5 * D_MODEL ** -0.5),
        "b_ada": nrm(ks[6], (DEPTH, 6 * D_MODEL), 0.02),
        "w_in": nrm(ks[7], (DEPTH, D_MODEL, N_IN), D_MODEL ** -0.5),
        "mlstm_conv_w": nrm(ks[8], (DEPTH, ML_CONV, 2 * ML_HEADS * ML_DQK), 0.5),
        "mlstm_conv_b": nrm(ks[9], (DEPTH, 2 * ML_HEADS * ML_DQK), 0.02),
        "mlstm_gate_b": gate_b,
        "mlstm_norm_w": 1.0 + nrm(ks[12], (DEPTH, ML_HEADS, ML_DV), 0.02),
        "q_norm_w": 1.0 + nrm(ks[13], (DEPTH, DA_DH), 0.02),
        "k_norm_w": 1.0 + nrm(ks[14], (DEPTH, DA_DH), 0.02),
        "lambda_q1": nrm(ks[15], (DEPTH, DA_DH), 0.1),
        "lambda_k1": nrm(ks[16], (DEPTH, DA_DH), 0.1),
        "lambda_q2": nrm(ks[17], (DEPTH, DA_DH), 0.1),
        "lambda_k2": nrm(ks[18], (DEPTH, DA_DH), 0.1),
        "subln_w": 1.0 + nrm(ks[19], (DEPTH, DA_DV), 0.02),
        "w_out": nrm(ks[20], (DEPTH, MIX_WIDTH, D_MODEL), MIX_WIDTH ** -0.5),
        "w_gate_up": nrm(ks[21], (DEPTH, D_MODEL, 2 * D_FF), D_MODEL ** -0.5),
        "w_down": nrm(ks[22], (DEPTH, D_FF, D_MODEL), D_FF ** -0.5),
    }


def reference(x, c, positions, norm1_w, norm2_w, w_ada, b_ada, w_in, mlstm_conv_w, mlstm_conv_b,
              mlstm_gate_b, mlstm_norm_w, q_norm_w, k_norm_w, lambda_q1, lambda_k1, lambda_q2,
              lambda_k2, subln_w, w_out, w_gate_up, w_down):
    B, S, _ = x.shape
    split_pts = np.cumsum(IN_SPLITS)[:-1].tolist()
    qk_w = ML_HEADS * ML_DQK
    for layer in range(DEPTH):
        lambda_init = 0.8 - 0.6 * math.exp(-0.3 * layer)
        mod = jnp.einsum('bd,de->be', jax.nn.silu(c), w_ada[layer]) + b_ada[layer]
        shift1, scale1, gate1, shift2, scale2, gate2 = jnp.split(mod[:, None, :], 6, axis=-1)

        h = rms_norm(x, norm1_w[layer]) * (1 + scale1) + shift1
        proj = h @ w_in[layer]
        mq, mk, mv, mo, mi, mf, dq, dk, dv = jnp.split(proj, split_pts, axis=-1)

        mqk = jax.nn.silu(causal_dwconv(jnp.concatenate([mq, mk], axis=-1), mlstm_conv_w[layer], mlstm_conv_b[layer]))
        mq, mk = mqk[..., :qk_w], mqk[..., qk_w:]
        gates = jnp.concatenate([mi, mf], axis=-1).astype(jnp.float32) + mlstm_gate_b[layer].astype(jnp.float32)
        gates = GATE_CAP * jnp.tanh(gates / GATE_CAP)
        li = jnp.transpose(gates[..., :ML_HEADS], (0, 2, 1))
        lf = jnp.transpose(jax.nn.log_sigmoid(gates[..., ML_HEADS:]), (0, 2, 1))
        to_heads = lambda t, d: jnp.transpose(t.reshape(B, S, ML_HEADS, d), (0, 2, 1, 3))
        h_ml = mlstm_chunkwise(to_heads(mq, ML_DQK), to_heads(mk, ML_DQK), to_heads(mv, ML_DV), li, lf)
        h_ml = rms_norm(jnp.transpose(h_ml, (0, 2, 1, 3)), mlstm_norm_w[layer])
        h_ml = h_ml.reshape(B, S, ML_WIDTH).astype(x.dtype) * jax.nn.sigmoid(mo)

        dq = partial_rope(rms_norm(dq.reshape(B, S, DA_HEADS, 2, DA_DH), q_norm_w[layer]), positions)
        dk = partial_rope(rms_norm(dk.reshape(B, S, DA_HEADS, 2, DA_DH), k_norm_w[layer]), positions)
        dq = jnp.transpose(dq, (0, 2, 3, 1, 4))
        dk = jnp.transpose(dk, (0, 2, 3, 1, 4))
        dv = jnp.transpose(dv.reshape(B, S, DA_HEADS, DA_DV), (0, 2, 1, 3))
        lam = (jnp.exp(jnp.sum(lambda_q1[layer].astype(jnp.float32) * lambda_k1[layer].astype(jnp.float32)))
               - jnp.exp(jnp.sum(lambda_q2[layer].astype(jnp.float32) * lambda_k2[layer].astype(jnp.float32)))
               + lambda_init)
        o_da = diff_attention(dq, dk, dv, lam)
        o_da = rms_norm(o_da, subln_w[layer]) * (1.0 - lambda_init)
        o_da = jnp.transpose(o_da, (0, 2, 1, 3)).reshape(B, S, DA_WIDTH)

        mix = jnp.concatenate([h_ml, o_da], axis=-1) @ w_out[layer]
        x = x + gate1 * mix

        h = rms_norm(x, norm2_w[layer]) * (1 + scale2) + shift2
        g, u = jnp.split(h @ w_gate_up[layer], 2, axis=-1)
        x = x + gate2 * ((jax.nn.silu(g) * u) @ w_down[layer])
    return x
```

```python
import functools
import math

import jax
import jax.numpy as jnp
import numpy as np
from jax import lax
from jax.experimental import pallas as pl
from jax.experimental.pallas import tpu as pltpu

F32 = jnp.float32
BF16 = jnp.bfloat16

D_MODEL = 2048
CHUNK = 64
ML_HEADS = 4
ML_DQK = 128
ML_DV = 256
ML_CONV = 4
GATE_CAP = 15.0
DA_HEADS = 8
DA_DH = 64
DA_DV = 128
ROPE_DIM = 16
ROPE_THETA = 500000.0
ML_WIDTH = ML_HEADS * ML_DV
DA_WIDTH = DA_HEADS * DA_DV
QK_W = ML_HEADS * ML_DQK
D_FF = 5632
EPS = 1e-6
NEG = -1e30

LANES = 128
VMEM_LIMIT = 52 * 1024 * 1024

P_COLS = 6144
GATE_LO = 3 * 1024
GATE_HI = GATE_LO + 2 * ML_HEADS

ML_L = 256
AT_T = 512


def _cparams(sem):
    return pltpu.CompilerParams(dimension_semantics=sem, vmem_limit_bytes=VMEM_LIMIT)


def _dot(a, b):
    return jnp.dot(a, b, preferred_element_type=F32)


def _split3(x):
    hi = x.astype(BF16)
    r1 = x - hi.astype(F32)
    mid = r1.astype(BF16)
    lo = (r1 - mid.astype(F32)).astype(BF16)
    return hi, mid, lo


def _ada_kernel(c_ref, w_ref, b_ref, o_ref):
    c = c_ref[...]
    a = (c * jax.nn.sigmoid(c)).astype(BF16)
    o_ref[...] = _dot(a, w_ref[...].astype(BF16)) + b_ref[...]


def _ada(c8, w, b):
    n = w.shape[1]
    tn = 1024
    return pl.pallas_call(
        _ada_kernel,
        out_shape=jax.ShapeDtypeStruct((8, n), F32),
        grid=(n // tn,),
        in_specs=[pl.BlockSpec((8, D_MODEL), lambda j: (0, 0)),
                  pl.BlockSpec((D_MODEL, tn), lambda j: (0, j)),
                  pl.BlockSpec((1, tn), lambda j: (0, j))],
        out_specs=pl.BlockSpec((8, tn), lambda j: (0, j)),
        compiler_params=_cparams(("parallel",)),
        name="adaln",
    )(c8, w, b)


def _norm_mod_rows(x_ref, nw_ref, sc_ref, sh_ref, emit, tm, sub=128):
    nw = nw_ref[...]
    sc1 = 1.0 + sc_ref[...]
    sh = sh_ref[...]

    def body(r, carry):
        off = pl.multiple_of(r * sub, sub)
        x = x_ref[pl.ds(off, sub), :]
        ms = jnp.mean(x * x, axis=-1, keepdims=True)
        h = (x * lax.rsqrt(ms + EPS)) * nw * sc1 + sh
        emit(off, h)
        return carry

    lax.fori_loop(0, tm // sub, body, 0)


def _inproj_kernel(x_ref, nw_ref, sc_ref, sh_ref, w_ref, wg_ref, o_ref, g_ref, h_ref, *, tm, sub):
    j = pl.program_id(1)

    @pl.when(j == 0)
    def _():
        wg = wg_ref[...]
        def emit(off, h):
            hh = h.astype(BF16)
            hl = (h - hh.astype(F32)).astype(BF16)
            h_ref[pl.ds(off, sub), :] = hh
            g_ref[pl.ds(off, sub), :] = (_dot(hh, wg[0]) + _dot(hh, wg[1]) + _dot(hl, wg[0]))
        _norm_mod_rows(x_ref, nw_ref, sc_ref, sh_ref, emit, tm, sub)

    o_ref[...] = _dot(h_ref[...], w_ref[...]).astype(BF16)


def _inproj(x2, nw, mod3, w_bf, wg2, tpb):
    t = x2.shape[0]
    tm, tn, sub = 1024, 1024, 128
    tiles_per_batch = tpb // tm
    kern = functools.partial(_inproj_kernel, tm=tm, sub=sub)
    return pl.pallas_call(
        kern,
        out_shape=(jax.ShapeDtypeStruct((t, P_COLS), BF16), jax.ShapeDtypeStruct((t, LANES), F32)),
        grid=(t // tm, P_COLS // tn),
        in_specs=[pl.BlockSpec((tm, D_MODEL), lambda i, j: (i, 0)),
                  pl.BlockSpec((1, D_MODEL), lambda i, j: (0, 0)),
                  pl.BlockSpec((None, 1, D_MODEL), lambda i, j: ((i // tiles_per_batch) * 6 + 1, 0, 0)),
                  pl.BlockSpec((None, 1, D_MODEL), lambda i, j: ((i // tiles_per_batch) * 6 + 0, 0, 0)),
                  pl.BlockSpec((D_MODEL, tn), lambda i, j: (0, j)),
                  pl.BlockSpec((2, D_MODEL, LANES), lambda i, j: (0, 0, 0))],
        out_specs=(pl.BlockSpec((tm, tn), lambda i, j: (i, j)),
                   pl.BlockSpec((tm, LANES), lambda i, j: (i, 0))),
        scratch_shapes=[pltpu.VMEM((tm, D_MODEL), BF16)],
        compiler_params=_cparams(("parallel", "arbitrary")),
        name="inproj",
    )(x2, nw, mod3, mod3, w_bf, wg2)


def _conv_silu(x, tail8, w4, b, row8):
    acc = b + w4[3:4, :] * x
    for k in (1, 2, 3):
        xr = pltpu.roll(x, k, 0)
        tr = pltpu.roll(tail8, k, 0)
        first = jnp.where(row8 < k, tr, xr[0:8, :])
        xs = jnp.concatenate([first, xr[8:, :]], axis=0)
        acc = acc + w4[3 - k:4 - k, :] * xs
    return acc * jax.nn.sigmoid(acc)


def _mlstm_kernel(qk_ref, v_ref, og_ref, g_ref, cw_ref, cb_ref, gb_ref, nw_ref, o_ref, ct_ref, tail_ref):
    L = ML_L
    c = pl.program_id(1)

    @pl.when(c == 0)
    def _():
        ct_ref[...] = jnp.zeros_like(ct_ref)
        tail_ref[...] = jnp.zeros_like(tail_ref)

    g = g_ref[...] + gb_ref[...]
    gc = GATE_CAP * jnp.tanh(g * (1.0 / GATE_CAP))
    lf_all = -jnp.log1p(jnp.exp(-gc))
    ri = lax.broadcasted_iota(jnp.int32, (L, L), 0)
    ci = lax.broadcasted_iota(jnp.int32, (L, L), 1)
    tri = ci <= ri
    trib = jnp.where(tri, 1.0, 0.0).astype(BF16)
    f_hi, f_mid, f_lo = _split3(lf_all)
    b_all = _dot(trib, f_hi) + _dot(trib, f_mid) + _dot(trib, f_lo)
    u_all = gc - pltpu.roll(b_all, LANES - ML_HEADS, 1)
    u_t = u_all.T

    row8 = lax.broadcasted_iota(jnp.int32, (8, LANES), 0)
    ones_b = jnp.ones((L, LANES), BF16)
    tails = tail_ref[...]

    for h in range(ML_HEADS):
        qs = slice(h * ML_DQK, (h + 1) * ML_DQK)
        ks = slice(QK_W + h * ML_DQK, QK_W + (h + 1) * ML_DQK)
        vs = slice(h * ML_DV, (h + 1) * ML_DV)
        q = _conv_silu(qk_ref[:, qs].astype(F32), tails[:, qs], cw_ref[:, qs], cb_ref[:, qs], row8)
        k = _conv_silu(qk_ref[:, ks].astype(F32), tails[:, ks], cw_ref[:, ks], cb_ref[:, ks], row8)
        q = q * (ML_DQK ** -0.5)

        b_col = jnp.broadcast_to(b_all[:, ML_HEADS + h:ML_HEADS + h + 1], (L, LANES))
        u_col = jnp.broadcast_to(u_all[:, h:h + 1], (L, LANES))
        b_end = b_all[L - 1:L, ML_HEADS + h:ML_HEADS + h + 1]
        u_row = u_t[h:h + 1, :]

        logd = jnp.concatenate([b_col, b_col], axis=1) + u_row
        d = jnp.exp(jnp.where(tri, logd, NEG))
        s = lax.dot_general(q.astype(BF16), k.astype(BF16), (((1,), (1,)), ((), ())),
                            preferred_element_type=F32) * d

        vaug = jnp.concatenate([v_ref[:, vs], ones_b], axis=1)
        ct = ct_ref[h]
        tot = _dot((q * jnp.exp(b_col)).astype(BF16), ct.astype(BF16)) + _dot(s.astype(BF16), vaug)
        den = tot[:, ML_DV:]
        den2 = jnp.concatenate([den, den], axis=1)
        hc = tot[:, :ML_DV] / jnp.maximum(jnp.abs(den2), 1.0)

        kw = (k * jnp.exp(b_end + u_col)).astype(BF16)
        upd = lax.dot_general(kw, vaug, (((0,), (0,)), ((), ())), preferred_element_type=F32)
        ct_ref[h] = jnp.exp(b_end) * ct + upd

        ms = jnp.mean(hc * hc, axis=-1, keepdims=True)
        hn = hc * lax.rsqrt(ms + EPS) * nw_ref[:, vs]
        o_ref[:, vs] = (hn * jax.nn.sigmoid(og_ref[:, vs].astype(F32))).astype(BF16)

    tail_ref[...] = qk_ref[L - 16:L, :].astype(F32)[8:16, :]


def _mlstm(p, g, cw, cb, gb, nw, batch, seq):
    t = p.shape[0]
    L = ML_L
    nc = seq // L
    return pl.pallas_call(
        _mlstm_kernel,
        out_shape=jax.ShapeDtypeStruct((t, ML_WIDTH), BF16),
        grid=(batch, nc),
        in_specs=[pl.BlockSpec((L, 1024), lambda b, c: (b * nc + c, 0)),
                  pl.BlockSpec((L, 1024), lambda b, c: (b * nc + c, 1)),
                  pl.BlockSpec((L, 1024), lambda b, c: (b * nc + c, 2)),
                  pl.BlockSpec((L, LANES), lambda b, c: (b * nc + c, 0)),
                  pl.BlockSpec((ML_CONV, 1024), lambda b, c: (0, 0)),
                  pl.BlockSpec((1, 1024), lambda b, c: (0, 0)),
                  pl.BlockSpec((1, LANES), lambda b, c: (0, 0)),
                  pl.BlockSpec((1, ML_WIDTH), lambda b, c: (0, 0))],
        out_specs=pl.BlockSpec((L, ML_WIDTH), lambda b, c: (b * nc + c, 0)),
        scratch_shapes=[pltpu.VMEM((ML_HEADS, ML_DQK, ML_DV + LANES), F32),
                        pltpu.VMEM((8, 1024), F32)],
        compiler_params=_cparams(("parallel", "arbitrary")),
        name="mlstm",
    )(p, p, p, g, cw, cb, gb, nw)


def _qkprep_kernel(pos_ref, q_ref, k_ref, invf_ref, qw_ref, qpw_ref, kw_ref, kpw_ref, qo_ref, ko_ref):
    pos = pos_ref[...].astype(F32)
    ang = pos * invf_ref[...]
    cs = jnp.cos(ang)
    sn = jnp.sin(ang)
    lane = lax.broadcasted_iota(jnp.int32, (1, LANES), 1)
    dl = lane & (DA_DH - 1)
    half = ROPE_DIM // 2
    sgn = jnp.where(dl < half, -1.0, jnp.where(dl < ROPE_DIM, 1.0, 0.0))
    sn = sn * sgn
    ri = lax.broadcasted_iota(jnp.int32, (LANES, LANES), 0)
    ci = lax.broadcasted_iota(jnp.int32, (LANES, LANES), 1)
    dc = ci & (DA_DH - 1)
    partner = jnp.where(dc < half, ci + half, jnp.where(dc < ROPE_DIM, ci - half, -1))
    perm = jnp.where(ri == partner, 1.0, 0.0).astype(BF16)
    ones_blk = jnp.where((ri >> 6) == (ci >> 6), 1.0, 0.0).astype(BF16)

    def prep(x_ref, w_ref, pw_ref, o_ref, scale):
        a = (w_ref[...] * scale) * cs
        bc = (pw_ref[...] * scale) * sn
        for hh in range(DA_HEADS):
            cols = slice(hh * LANES, (hh + 1) * LANES)
            xb = x_ref[:, cols]
            xf = xb.astype(F32)
            sq = xf * xf
            sqh = sq.astype(BF16)
            sql = (sq - sqh.astype(F32)).astype(BF16)
            ms = (_dot(sqh, ones_blk) + _dot(sql, ones_blk)) * (1.0 / DA_DH)
            r = lax.rsqrt(ms + EPS)
            px = _dot(xb, perm)
            o_ref[:, cols] = (r * (xf * a + px * bc)).astype(BF16)

    prep(q_ref, qw_ref, qpw_ref, qo_ref, DA_DH ** -0.5)
    prep(k_ref, kw_ref, kpw_ref, ko_ref, 1.0)


def _qkprep(pos2, p, invf, qw, qpw, kw, kpw):
    t = p.shape[0]
    tm = 512
    vec = pl.BlockSpec((1, LANES), lambda i: (0, 0))
    return pl.pallas_call(
        _qkprep_kernel,
        out_shape=(jax.ShapeDtypeStruct((t, 1024), BF16), jax.ShapeDtypeStruct((t, 1024), BF16)),
        grid=(t // tm,),
        in_specs=[pl.BlockSpec((tm, 1), lambda i: (i, 0)),
                  pl.BlockSpec((tm, 1024), lambda i: (i, 3)),
                  pl.BlockSpec((tm, 1024), lambda i: (i, 4)),
                  vec, vec, vec, vec, vec],
        out_specs=(pl.BlockSpec((tm, 1024), lambda i: (i, 0)),
                   pl.BlockSpec((tm, 1024), lambda i: (i, 0))),
        compiler_params=_cparams(("parallel",)),
        name="qkprep",
    )(pos2, p, p, invf, qw, qpw, kw, kpw)


def _attn_kernel(q_ref, k_ref, v_ref, lq1_ref, lk1_ref, lq2_ref, lk2_ref, sw_ref, o_ref, m_ref, acc_ref,
                 *, lambda_init):
    T = AT_T
    qi = pl.program_id(2)
    q = q_ref[...]
    lane = lax.broadcasted_iota(jnp.int32, (T, LANES), 1)
    zero = jnp.zeros_like(q)
    q2 = jnp.concatenate([jnp.where(lane < DA_DH, q, zero), jnp.where(lane >= DA_DH, q, zero)], axis=0)
    m_ref[...] = jnp.full(m_ref.shape, NEG, F32)
    acc_ref[...] = jnp.zeros(acc_ref.shape, F32)
    ones_b = jnp.ones((T, LANES), BF16)

    def step(j, masked):
        off = pl.multiple_of(j * T, T)
        kb = k_ref[pl.ds(off, T), :]
        vb = v_ref[pl.ds(off, T), :]
        s = lax.dot_general(q2, kb, (((1,), (1,)), ((), ())), preferred_element_type=F32)
        if masked:
            r = lax.broadcasted_iota(jnp.int32, (2 * T, T), 0)
            cc = lax.broadcasted_iota(jnp.int32, (2 * T, T), 1)
            rr = jnp.where(r >= T, r - T, r)
            s = jnp.where((cc >> 6) <= (rr >> 6), s, NEG)
        m_old = m_ref[...]
        m_new = jnp.maximum(m_old, jnp.max(s, axis=1, keepdims=True))
        alpha = jnp.exp(m_old - m_new)
        p = jnp.exp(s - jnp.concatenate([m_new] * (T // LANES), axis=1))
        pv = _dot(p.astype(BF16), jnp.concatenate([vb, ones_b], axis=1))
        acc_ref[...] = acc_ref[...] * jnp.concatenate([alpha, alpha], axis=1) + pv
        m_ref[...] = m_new

    def body(j, carry):
        step(j, False)
        return carry

    lax.fori_loop(0, qi, body, 0)
    step(qi, True)

    acc = acc_ref[...]
    o0 = acc[:T, :DA_DV] / acc[:T, DA_DV:]
    o1 = acc[T:, :DA_DV] / acc[T:, DA_DV:]
    lam = (jnp.exp(jnp.sum(lq1_ref[...] * lk1_ref[...], axis=-1, keepdims=True))
           - jnp.exp(jnp.sum(lq2_ref[...] * lk2_ref[...], axis=-1, keepdims=True)) + lambda_init)
    o = o0 - lam * o1
    ms = jnp.mean(o * o, axis=-1, keepdims=True)
    o_ref[...] = (o * lax.rsqrt(ms + EPS) * (sw_ref[...] * (1.0 - lambda_init))).astype(BF16)


def _attn(qn, kn, p, lq1, lk1, lq2, lk2, sw, batch, seq, lambda_init):
    t = qn.shape[0]
    T = AT_T
    nq = seq // T
    vcol0 = 5 * 1024 // LANES
    vec64 = pl.BlockSpec((1, DA_DH), lambda b, h, i: (0, 0))
    kern = functools.partial(_attn_kernel, lambda_init=lambda_init)
    return pl.pallas_call(
        kern,
        out_shape=jax.ShapeDtypeStruct((t, DA_WIDTH), BF16),
        grid=(batch, DA_HEADS, nq),
        in_specs=[pl.BlockSpec((T, LANES), lambda b, h, i: (b * nq + i, h)),
                  pl.BlockSpec((seq, LANES), lambda b, h, i: (b, h)),
                  pl.BlockSpec((seq, LANES), lambda b, h, i: (b, vcol0 + h)),
                  vec64, vec64, vec64, vec64,
                  pl.BlockSpec((1, DA_DV), lambda b, h, i: (0, 0))],
        out_specs=pl.BlockSpec((T, LANES), lambda b, h, i: (b * nq + i, h)),
        scratch_shapes=[pltpu.VMEM((2 * T, LANES), F32), pltpu.VMEM((2 * T, 2 * LANES), F32)],
        compiler_params=_cparams(("parallel", "parallel", "arbitrary")),
        name="diffattn",
    )(qn, kn, p, lq1, lk1, lq2, lk2, sw)


def _outproj_kernel(x_ref, g_ref, a_ref, b_ref, wa_ref, wb_ref, o_ref):
    mix = _dot(a_ref[...], wa_ref[...]) + _dot(b_ref[...], wb_ref[...])
    o_ref[...] = x_ref[...] + g_ref[...] * mix


def _outproj(x2, mod3, hml, oda, w_bf, tpb):
    t = x2.shape[0]
    tm, tn = 1024, 1024
    tiles_per_batch = tpb // tm
    nb = D_MODEL // tn
    return pl.pallas_call(
        _outproj_kernel,
        out_shape=jax.ShapeDtypeStruct((t, D_MODEL), F32),
        grid=(t // tm, nb),
        in_specs=[pl.BlockSpec((tm, tn), lambda i, j: (i, j)),
                  pl.BlockSpec((None, 1, tn), lambda i, j: ((i // tiles_per_batch) * 6 * nb + 2 * nb + j, 0, 0)),
                  pl.BlockSpec((tm, ML_WIDTH), lambda i, j: (i, 0)),
                  pl.BlockSpec((tm, DA_WIDTH), lambda i, j: (i, 0)),
                  pl.BlockSpec((ML_WIDTH, tn), lambda i, j: (0, j)),
                  pl.BlockSpec((DA_WIDTH, tn), lambda i, j: (1, j))],
        out_specs=pl.BlockSpec((tm, tn), lambda i, j: (i, j)),
        compiler_params=_cparams(("parallel", "parallel")),
        name="outproj",
    )(x2, mod3.reshape(-1, 1, tn), hml, oda, w_bf, w_bf)


def _ffn_kernel(x_ref, nw_ref, sc_ref, sh_ref, gt_ref, wg_ref, wu_ref, wd_ref, o_ref, h_ref, acc_ref,
                *, tm, sub, nj):
    j = pl.program_id(1)

    @pl.when(j == 0)
    def _():
        def emit(off, h):
            h_ref[pl.ds(off, sub), :] = h.astype(BF16)
        _norm_mod_rows(x_ref, nw_ref, sc_ref, sh_ref, emit, tm, sub)
        acc_ref[...] = jnp.zeros_like(acc_ref)

    h = h_ref[...]
    g = _dot(h, wg_ref[...])
    u = _dot(h, wu_ref[...])
    a = (g * jax.nn.sigmoid(g) * u).astype(BF16)
    acc_ref[...] += _dot(a, wd_ref[...])

    @pl.when(j == nj - 1)
    def _():
        o_ref[...] = x_ref[...] + gt_ref[...] * acc_ref[...]


def _ffn(x1, nw, mod3, wgu_bf, wd_bf, tpb):
    t = x1.shape[0]
    tm, tf, sub = 512, 512, 128
    tiles_per_batch = tpb // tm
    nj = D_FF // tf
    kern = functools.partial(_ffn_kernel, tm=tm, sub=sub, nj=nj)
    mrow = lambda k: pl.BlockSpec((None, 1, D_MODEL), lambda i, j: ((i // tiles_per_batch) * 6 + k, 0, 0))
    return pl.pallas_call(
        kern,
        out_shape=jax.ShapeDtypeStruct((t, D_MODEL), F32),
        grid=(t // tm, nj),
        in_specs=[pl.BlockSpec((tm, D_MODEL), lambda i, j: (i, 0)),
                  pl.BlockSpec((1, D_MODEL), lambda i, j: (0, 0)),
                  mrow(4), mrow(3), mrow(5),
                  pl.BlockSpec((D_MODEL, tf), lambda i, j: (0, j)),
                  pl.BlockSpec((D_MODEL, tf), lambda i, j: (0, nj + j)),
                  pl.BlockSpec((tf, D_MODEL), lambda i, j: (j, 0))],
        out_specs=pl.BlockSpec((tm, D_MODEL), lambda i, j: (i, 0)),
        scratch_shapes=[pltpu.VMEM((tm, D_MODEL), BF16), pltpu.VMEM((tm, D_MODEL), F32)],
        compiler_params=_cparams(("parallel", "arbitrary")),
        name="ffn",
    )(x1, nw, mod3, mod3, mod3, wgu_bf, wgu_bf, wd_bf)


def _rope_partner_index():
    idx = np.arange(LANES)
    d = idx % DA_DH
    half = ROPE_DIM // 2
    return np.where(d < half, idx + half, np.where(d < ROPE_DIM, idx - half, idx))


def kernel(x, c, positions, norm1_w, norm2_w, w_ada, b_ada, w_in, mlstm_conv_w, mlstm_conv_b, mlstm_gate_b,
           mlstm_norm_w, q_norm_w, k_norm_w, lambda_q1, lambda_k1, lambda_q2, lambda_k2, subln_w, w_out,
           w_gate_up, w_down):
    B, S, D = x.shape
    T = B * S
    depth = w_in.shape[0]
    xf = x.reshape(T, D)
    c8 = jnp.pad(c, ((0, 8 - B), (0, 0)))
    pos2 = positions.reshape(T, 1)

    inv_freq = ROPE_THETA ** (-jnp.arange(0, ROPE_DIM, 2, dtype=F32) / ROPE_DIM)
    lane = np.arange(LANES)
    dl = lane % DA_DH
    invf = jnp.where(jnp.asarray(dl < ROPE_DIM), inv_freq[dl % (ROPE_DIM // 2)], 0.0).reshape(1, LANES)
    pidx = _rope_partner_index()

    for layer in range(depth):
        lambda_init = 0.8 - 0.6 * math.exp(-0.3 * layer)
        mod = _ada(c8, w_ada[layer], b_ada[layer].reshape(1, -1))
        mod3 = mod[:B].reshape(B * 6, 1, D)

        w_l = w_in[layer]
        w_bf = jnp.concatenate([w_l[:, :GATE_LO], w_l[:, GATE_HI:]], axis=1).astype(BF16)
        wgate = jnp.pad(w_l[:, GATE_LO:GATE_HI], ((0, 0), (0, LANES - 2 * ML_HEADS)))
        wg_hi = wgate.astype(BF16)
        wg_lo = (wgate - wg_hi.astype(F32)).astype(BF16)
        wg2 = jnp.stack([wg_hi, wg_lo])

        p, g = _inproj(xf, norm1_w[layer].reshape(1, D), mod3, w_bf, wg2, S)

        gb = jnp.pad(mlstm_gate_b[layer].reshape(1, -1), ((0, 0), (0, LANES - 2 * ML_HEADS)))
        hml = _mlstm(p, g, mlstm_conv_w[layer], mlstm_conv_b[layer].reshape(1, -1), gb,
                     mlstm_norm_w[layer].reshape(1, ML_WIDTH), B, S)

        qw = jnp.tile(q_norm_w[layer], 2).reshape(1, LANES)
        kw = jnp.tile(k_norm_w[layer], 2).reshape(1, LANES)
        qn, kn = _qkprep(pos2, p, invf, qw, qw[:, pidx], kw, kw[:, pidx])

        r64 = lambda a: a[layer].reshape(1, DA_DH).astype(F32)
        oda = _attn(qn, kn, p, r64(lambda_q1), r64(lambda_k1), r64(lambda_q2), r64(lambda_k2),
                    subln_w[layer].reshape(1, DA_DV), B, S, lambda_init)

        x1 = _outproj(xf, mod3, hml, oda, w_out[layer].astype(BF16), S)
        xf = _ffn(x1, norm2_w[layer].reshape(1, D), mod3, w_gate_up[layer].astype(BF16),
                  w_down[layer].astype(BF16), S)
    return xf.reshape(B, S, D)
```

```python
import functools
import math

import jax
import jax.numpy as jnp
import numpy as np
from jax import lax
from jax.experimental import pallas as pl
from jax.experimental.pallas import tpu as pltpu

F32 = jnp.float32
BF16 = jnp.bfloat16

D_MODEL = 2048
CHUNK = 64
ML_HEADS = 4
ML_DQK = 128
ML_DV = 256
ML_CONV = 4
GATE_CAP = 15.0
DA_HEADS = 8
DA_DH = 64
DA_DV = 128
ROPE_DIM = 16
ROPE_THETA = 500000.0
ML_WIDTH = ML_HEADS * ML_DV
DA_WIDTH = DA_HEADS * DA_DV
QK_W = ML_HEADS * ML_DQK
D_FF = 5632
EPS = 1e-6
NEG = -1e30
LOG2E = 1.4426950408889634

LANES = 128
VMEM_LIMIT = 52 * 1024 * 1024

P_COLS = 6144
GATE_LO = 3 * 1024
GATE_HI = GATE_LO + 2 * ML_HEADS

ML_L = 256
AT_T = 512


def _cparams(sem):
    return pltpu.CompilerParams(dimension_semantics=sem, vmem_limit_bytes=VMEM_LIMIT)


def _dot(a, b):
    return jnp.dot(a, b, preferred_element_type=F32)


def _split3(x):
    hi = x.astype(BF16)
    r1 = x - hi.astype(F32)
    mid = r1.astype(BF16)
    lo = (r1 - mid.astype(F32)).astype(BF16)
    return hi, mid, lo


def _ada_kernel(c_ref, w_ref, b_ref, o_ref):
    c = c_ref[...]
    a = (c * jax.nn.sigmoid(c)).astype(BF16)
    o_ref[...] = _dot(a, w_ref[...].astype(BF16)) + b_ref[...]


def _ada(c8, w, b):
    n = w.shape[1]
    tn = 1024
    return pl.pallas_call(
        _ada_kernel,
        out_shape=jax.ShapeDtypeStruct((8, n), F32),
        grid=(n // tn,),
        in_specs=[pl.BlockSpec((8, D_MODEL), lambda j: (0, 0)),
                  pl.BlockSpec((D_MODEL, tn), lambda j: (0, j)),
                  pl.BlockSpec((1, tn), lambda j: (0, j))],
        out_specs=pl.BlockSpec((8, tn), lambda j: (0, j)),
        compiler_params=_cparams(("parallel",)),
        name="adaln",
    )(c8, w, b)


def _norm_mod_rows(x_ref, nw_ref, sc_ref, sh_ref, emit, tm, sub=128):
    nw = nw_ref[...]
    sc1 = 1.0 + sc_ref[...]
    sh = sh_ref[...]

    def body(r, carry):
        off = pl.multiple_of(r * sub, sub)
        x = x_ref[pl.ds(off, sub), :]
        ms = jnp.mean(x * x, axis=-1, keepdims=True)
        h = (x * lax.rsqrt(ms + EPS)) * nw * sc1 + sh
        emit(off, h)
        return carry

    lax.fori_loop(0, tm // sub, body, 0)


def _inproj_kernel(x_ref, nw_ref, sc_ref, sh_ref, w_ref, wg_ref, o_ref, g_ref, h_ref, *, tm, sub):
    j = pl.program_id(1)

    @pl.when(j == 0)
    def _():
        wg = wg_ref[...]
        def emit(off, h):
            hh = h.astype(BF16)
            hl = (h - hh.astype(F32)).astype(BF16)
            h_ref[pl.ds(off, sub), :] = hh
            g_ref[pl.ds(off, sub), :] = (_dot(hh, wg[0]) + _dot(hh, wg[1]) + _dot(hl, wg[0]))
        _norm_mod_rows(x_ref, nw_ref, sc_ref, sh_ref, emit, tm, sub)

    o_ref[...] = _dot(h_ref[...], w_ref[...]).astype(BF16)


def _inproj(x2, nw, mod3, w_bf, wg2, tpb):
    t = x2.shape[0]
    tm, tn, sub = 1024, 1024, 128
    tiles_per_batch = tpb // tm
    kern = functools.partial(_inproj_kernel, tm=tm, sub=sub)
    return pl.pallas_call(
        kern,
        out_shape=(jax.ShapeDtypeStruct((t, P_COLS), BF16), jax.ShapeDtypeStruct((t, LANES), F32)),
        grid=(t // tm, P_COLS // tn),
        in_specs=[pl.BlockSpec((tm, D_MODEL), lambda i, j: (i, 0)),
                  pl.BlockSpec((1, D_MODEL), lambda i, j: (0, 0)),
                  pl.BlockSpec((None, 1, D_MODEL), lambda i, j: ((i // tiles_per_batch) * 6 + 1, 0, 0)),
                  pl.BlockSpec((None, 1, D_MODEL), lambda i, j: ((i // tiles_per_batch) * 6 + 0, 0, 0)),
                  pl.BlockSpec((D_MODEL, tn), lambda i, j: (0, j)),
                  pl.BlockSpec((2, D_MODEL, LANES), lambda i, j: (0, 0, 0))],
        out_specs=(pl.BlockSpec((tm, tn), lambda i, j: (i, j)),
                   pl.BlockSpec((tm, LANES), lambda i, j: (i, 0))),
        scratch_shapes=[pltpu.VMEM((tm, D_MODEL), BF16)],
        compiler_params=_cparams(("parallel", "arbitrary")),
        name="inproj",
    )(x2, nw, mod3, mod3, w_bf, wg2)


def _conv_silu(x, tail8, w4, b, row8):
    acc = b + w4[3:4, :] * x
    for k in (1, 2, 3):
        xr = pltpu.roll(x, k, 0)
        tr = pltpu.roll(tail8, k, 0)
        first = jnp.where(row8 < k, tr, xr[0:8, :])
        xs = jnp.concatenate([first, xr[8:, :]], axis=0)
        acc = acc + w4[3 - k:4 - k, :] * xs
    return acc * jax.nn.sigmoid(acc)


def _mlstm_kernel(qk_ref, v_ref, og_ref, g_ref, cw_ref, cb_ref, gb_ref, nw_ref, o_ref, ct_ref, tail_ref):
    L = ML_L
    c = pl.program_id(1)

    @pl.when(c == 0)
    def _():
        ct_ref[...] = jnp.zeros_like(ct_ref)
        tail_ref[...] = jnp.zeros_like(tail_ref)

    g = g_ref[...] + gb_ref[...]
    gc = GATE_CAP * jnp.tanh(g * (1.0 / GATE_CAP))
    lf_all = -jnp.log1p(jnp.exp(-gc))
    ri = lax.broadcasted_iota(jnp.int32, (L, L), 0)
    ci = lax.broadcasted_iota(jnp.int32, (L, L), 1)
    tri = ci <= ri
    trib = jnp.where(tri, 1.0, 0.0).astype(BF16)
    f_hi, f_mid, f_lo = _split3(lf_all)
    b_all = _dot(trib, f_hi) + _dot(trib, f_mid) + _dot(trib, f_lo)
    u_all = gc - pltpu.roll(b_all, LANES - ML_HEADS, 1)
    u_t = u_all.T

    row8 = lax.broadcasted_iota(jnp.int32, (8, LANES), 0)
    ones_b = jnp.ones((L, LANES), BF16)
    tails = tail_ref[...]

    for h in range(ML_HEADS):
        qs = slice(h * ML_DQK, (h + 1) * ML_DQK)
        ks = slice(QK_W + h * ML_DQK, QK_W + (h + 1) * ML_DQK)
        vs = slice(h * ML_DV, (h + 1) * ML_DV)
        q = _conv_silu(qk_ref[:, qs].astype(F32), tails[:, qs], cw_ref[:, qs], cb_ref[:, qs], row8)
        k = _conv_silu(qk_ref[:, ks].astype(F32), tails[:, ks], cw_ref[:, ks], cb_ref[:, ks], row8)
        q = q * (ML_DQK ** -0.5)

        b_col = jnp.broadcast_to(b_all[:, ML_HEADS + h:ML_HEADS + h + 1], (L, LANES))
        u_col = jnp.broadcast_to(u_all[:, h:h + 1], (L, LANES))
        b_end = b_all[L - 1:L, ML_HEADS + h:ML_HEADS + h + 1]
        u_row = u_t[h:h + 1, :]

        logd = jnp.concatenate([b_col, b_col], axis=1) + u_row
        d = jnp.exp(jnp.where(tri, logd, NEG))
        s = lax.dot_general(q.astype(BF16), k.astype(BF16), (((1,), (1,)), ((), ())),
                            preferred_element_type=F32) * d

        vaug = jnp.concatenate([v_ref[:, vs], ones_b], axis=1)
        ct = ct_ref[h]
        tot = _dot((q * jnp.exp(b_col)).astype(BF16), ct.astype(BF16)) + _dot(s.astype(BF16), vaug)
        den = tot[:, ML_DV:]
        den2 = jnp.concatenate([den, den], axis=1)
        hc = tot[:, :ML_DV] / jnp.maximum(jnp.abs(den2), 1.0)

        kw = (k * jnp.exp(b_end + u_col)).astype(BF16)
        upd = lax.dot_general(kw, vaug, (((0,), (0,)), ((), ())), preferred_element_type=F32)
        ct_ref[h] = jnp.exp(b_end) * ct + upd

        ms = jnp.mean(hc * hc, axis=-1, keepdims=True)
        hn = hc * lax.rsqrt(ms + EPS) * nw_ref[:, vs]
        o_ref[:, vs] = (hn * jax.nn.sigmoid(og_ref[:, vs].astype(F32))).astype(BF16)

    tail_ref[...] = qk_ref[L - 16:L, :].astype(F32)[8:16, :]


def _mlstm(p, g, cw, cb, gb, nw, batch, seq):
    t = p.shape[0]
    L = ML_L
    nc = seq // L
    return pl.pallas_call(
        _mlstm_kernel,
        out_shape=jax.ShapeDtypeStruct((t, ML_WIDTH), BF16),
        grid=(batch, nc),
        in_specs=[pl.BlockSpec((L, 1024), lambda b, c: (b * nc + c, 0)),
                  pl.BlockSpec((L, 1024), lambda b, c: (b * nc + c, 1)),
                  pl.BlockSpec((L, 1024), lambda b, c: (b * nc + c, 2)),
                  pl.BlockSpec((L, LANES), lambda b, c: (b * nc + c, 0)),
                  pl.BlockSpec((ML_CONV, 1024), lambda b, c: (0, 0)),
                  pl.BlockSpec((1, 1024), lambda b, c: (0, 0)),
                  pl.BlockSpec((1, LANES), lambda b, c: (0, 0)),
                  pl.BlockSpec((1, ML_WIDTH), lambda b, c: (0, 0))],
        out_specs=pl.BlockSpec((L, ML_WIDTH), lambda b, c: (b * nc + c, 0)),
        scratch_shapes=[pltpu.VMEM((ML_HEADS, ML_DQK, ML_DV + LANES), F32),
                        pltpu.VMEM((8, 1024), F32)],
        compiler_params=_cparams(("parallel", "arbitrary")),
        name="mlstm",
    )(p, p, p, g, cw, cb, gb, nw)


def _qkprep_kernel(pos_ref, q_ref, k_ref, invf_ref, qw_ref, qpw_ref, kw_ref, kpw_ref, qo_ref, ko_ref):
    pos = pos_ref[...].astype(F32)
    ang = pos * invf_ref[...]
    cs = jnp.cos(ang)
    sn = jnp.sin(ang)
    lane = lax.broadcasted_iota(jnp.int32, (1, LANES), 1)
    dl = lane & (DA_DH - 1)
    half = ROPE_DIM // 2
    sgn = jnp.where(dl < half, -1.0, jnp.where(dl < ROPE_DIM, 1.0, 0.0))
    sn = sn * sgn
    ri = lax.broadcasted_iota(jnp.int32, (LANES, LANES), 0)
    ci = lax.broadcasted_iota(jnp.int32, (LANES, LANES), 1)
    dc = ci & (DA_DH - 1)
    partner = jnp.where(dc < half, ci + half, jnp.where(dc < ROPE_DIM, ci - half, -1))
    perm = jnp.where(ri == partner, 1.0, 0.0).astype(BF16)
    ones_blk = jnp.where((ri >> 6) == (ci >> 6), 1.0, 0.0).astype(BF16)

    def prep(x_ref, w_ref, pw_ref, o_ref, scale):
        a = (w_ref[...] * scale) * cs
        bc = (pw_ref[...] * scale) * sn
        for hh in range(DA_HEADS):
            cols = slice(hh * LANES, (hh + 1) * LANES)
            xb = x_ref[:, cols]
            xf = xb.astype(F32)
            sq = xf * xf
            sqh = sq.astype(BF16)
            sql = (sq - sqh.astype(F32)).astype(BF16)
            ms = (_dot(sqh, ones_blk) + _dot(sql, ones_blk)) * (1.0 / DA_DH)
            r = lax.rsqrt(ms + EPS)
            px = _dot(xb, perm)
            o_ref[:, cols] = (r * (xf * a + px * bc)).astype(BF16)

    prep(q_ref, qw_ref, qpw_ref, qo_ref, DA_DH ** -0.5 * LOG2E)
    prep(k_ref, kw_ref, kpw_ref, ko_ref, 1.0)


def _qkprep(pos2, p, invf, qw, qpw, kw, kpw):
    t = p.shape[0]
    tm = 512
    vec = pl.BlockSpec((1, LANES), lambda i: (0, 0))
    return pl.pallas_call(
        _qkprep_kernel,
        out_shape=(jax.ShapeDtypeStruct((t, 1024), BF16), jax.ShapeDtypeStruct((t, 1024), BF16)),
        grid=(t // tm,),
        in_specs=[pl.BlockSpec((tm, 1), lambda i: (i, 0)),
                  pl.BlockSpec((tm, 1024), lambda i: (i, 3)),
                  pl.BlockSpec((tm, 1024), lambda i: (i, 4)),
                  vec, vec, vec, vec, vec],
        out_specs=(pl.BlockSpec((tm, 1024), lambda i: (i, 0)),
                   pl.BlockSpec((tm, 1024), lambda i: (i, 0))),
        compiler_params=_cparams(("parallel",)),
        name="qkprep",
    )(pos2, p, p, invf, qw, qpw, kw, kpw)


def _attn_kernel(q_ref, k_ref, v_ref, lq1_ref, lk1_ref, lq2_ref, lk2_ref, sw_ref, o_ref, m_ref, acc_ref,
                 sa_ref, sb_ref, *, lambda_init):
    T = AT_T
    qi = pl.program_id(2)
    q = q_ref[...]
    lane = lax.broadcasted_iota(jnp.int32, (T, LANES), 1)
    zero = jnp.zeros_like(q)
    qm = (jnp.where(lane < DA_DH, q, zero), jnp.where(lane >= DA_DH, q, zero))
    m_ref[...] = jnp.full(m_ref.shape, NEG, F32)
    acc_ref[...] = jnp.zeros(acc_ref.shape, F32)
    ones_b = jnp.ones((T, LANES), BF16)

    sbufs = (sa_ref, sb_ref)

    def score(j, buf, masked):
        off = pl.multiple_of(j * T, T)
        kb = k_ref[pl.ds(off, T), :]
        for mi in range(2):
            s = lax.dot_general(qm[mi], kb, (((1,), (1,)), ((), ())), preferred_element_type=F32)
            if masked:
                r = lax.broadcasted_iota(jnp.int32, (T, T), 0)
                cc = lax.broadcasted_iota(jnp.int32, (T, T), 1)
                s = jnp.where((cc >> 6) <= (r >> 6), s, NEG)
            sbufs[buf][mi] = s

    def consume(j, buf):
        off = pl.multiple_of(j * T, T)
        vaug = jnp.concatenate([v_ref[pl.ds(off, T), :], ones_b], axis=1)
        for mi in range(2):
            s = sbufs[buf][mi]
            m_old = m_ref[mi]
            m_new = jnp.maximum(m_old, jnp.max(s, axis=1, keepdims=True))
            al = jnp.exp2(m_old - m_new)
            p = jnp.exp2(s - jnp.concatenate([m_new] * (T // LANES), axis=1)).astype(BF16)
            acc_ref[mi] = acc_ref[mi] * jnp.concatenate([al, al], axis=1) + _dot(p, vaug)
            m_ref[mi] = m_new

    score(qi, 0, True)

    def pair(i, carry):
        score(2 * i, 1, False)
        consume(jnp.where(i == 0, qi, 2 * i - 1), 0)
        score(2 * i + 1, 0, False)
        consume(2 * i, 1)
        return carry

    npair = qi // 2
    lax.fori_loop(0, npair, pair, 0)
    last0 = jnp.where(npair == 0, qi, 2 * npair - 1)

    @pl.when(qi % 2 == 1)
    def _():
        score(qi - 1, 1, False)
        consume(last0, 0)
        consume(qi - 1, 1)

    @pl.when(qi % 2 == 0)
    def _():
        consume(last0, 0)

    acc0 = acc_ref[0]
    acc1 = acc_ref[1]
    o0 = acc0[:, :DA_DV] / acc0[:, DA_DV:]
    o1 = acc1[:, :DA_DV] / acc1[:, DA_DV:]
    lam = (jnp.exp(jnp.sum(lq1_ref[...] * lk1_ref[...], axis=-1, keepdims=True))
           - jnp.exp(jnp.sum(lq2_ref[...] * lk2_ref[...], axis=-1, keepdims=True)) + lambda_init)
    o = o0 - lam * o1
    ms = jnp.mean(o * o, axis=-1, keepdims=True)
    o_ref[...] = (o * lax.rsqrt(ms + EPS) * (sw_ref[...] * (1.0 - lambda_init))).astype(BF16)


def _attn(qn, kn, p, lq1, lk1, lq2, lk2, sw, batch, seq, lambda_init):
    t = qn.shape[0]
    T = AT_T
    nq = seq // T
    vcol0 = 5 * 1024 // LANES
    vec64 = pl.BlockSpec((1, DA_DH), lambda b, h, i: (0, 0))
    kern = functools.partial(_attn_kernel, lambda_init=lambda_init)
    return pl.pallas_call(
        kern,
        out_shape=jax.ShapeDtypeStruct((t, DA_WIDTH), BF16),
        grid=(batch, DA_HEADS, nq),
        in_specs=[pl.BlockSpec((T, LANES), lambda b, h, i: (b * nq + i, h)),
                  pl.BlockSpec((seq, LANES), lambda b, h, i: (b, h)),
                  pl.BlockSpec((seq, LANES), lambda b, h, i: (b, vcol0 + h)),
                  vec64, vec64, vec64, vec64,
                  pl.BlockSpec((1, DA_DV), lambda b, h, i: (0, 0))],
        out_specs=pl.BlockSpec((T, LANES), lambda b, h, i: (b * nq + i, h)),
        scratch_shapes=[pltpu.VMEM((2, T, LANES), F32), pltpu.VMEM((2, T, 2 * LANES), F32),
                        pltpu.VMEM((2, T, T), F32), pltpu.VMEM((2, T, T), F32)],
        compiler_params=_cparams(("parallel", "parallel", "arbitrary")),
        name="diffattn",
    )(qn, kn, p, lq1, lk1, lq2, lk2, sw)


def _outproj_kernel(x_ref, g_ref, a_ref, b_ref, wa_ref, wb_ref, o_ref):
    mix = _dot(a_ref[...], wa_ref[...]) + _dot(b_ref[...], wb_ref[...])
    o_ref[...] = x_ref[...] + g_ref[...] * mix


def _outproj(x2, mod3, hml, oda, w_bf, tpb):
    t = x2.shape[0]
    tm, tn = 1024, 1024
    tiles_per_batch = tpb // tm
    nb = D_MODEL // tn
    return pl.pallas_call(
        _outproj_kernel,
        out_shape=jax.ShapeDtypeStruct((t, D_MODEL), F32),
        grid=(t // tm, nb),
        in_specs=[pl.BlockSpec((tm, tn), lambda i, j: (i, j)),
                  pl.BlockSpec((None, 1, tn), lambda i, j: ((i // tiles_per_batch) * 6 * nb + 2 * nb + j, 0, 0)),
                  pl.BlockSpec((tm, ML_WIDTH), lambda i, j: (i, 0)),
                  pl.BlockSpec((tm, DA_WIDTH), lambda i, j: (i, 0)),
                  pl.BlockSpec((ML_WIDTH, tn), lambda i, j: (0, j)),
                  pl.BlockSpec((DA_WIDTH, tn), lambda i, j: (1, j))],
        out_specs=pl.BlockSpec((tm, tn), lambda i, j: (i, j)),
        compiler_params=_cparams(("parallel", "parallel")),
        name="outproj",
    )(x2, mod3.reshape(-1, 1, tn), hml, oda, w_bf, w_bf)


def _ffn_kernel(x_ref, nw_ref, sc_ref, sh_ref, gt_ref, wg_ref, wu_ref, wd_ref, o_ref, h_ref, acc_ref,
                *, tm, sub, nj):
    j = pl.program_id(1)

    @pl.when(j == 0)
    def _():
        def emit(off, h):
            h_ref[pl.ds(off, sub), :] = h.astype(BF16)
        _norm_mod_rows(x_ref, nw_ref, sc_ref, sh_ref, emit, tm, sub)
        acc_ref[...] = jnp.zeros_like(acc_ref)

    h = h_ref[...]
    g = _dot(h, wg_ref[...])
    u = _dot(h, wu_ref[...])
    a = (g * jax.nn.sigmoid(g) * u).astype(BF16)
    acc_ref[...] += _dot(a, wd_ref[...])

    @pl.when(j == nj - 1)
    def _():
        o_ref[...] = x_ref[...] + gt_ref[...] * acc_ref[...]


def _ffn(x1, nw, mod3, wgu_bf, wd_bf, tpb):
    t = x1.shape[0]
    tm, tf, sub = 512, 512, 128
    tiles_per_batch = tpb // tm
    nj = D_FF // tf
    kern = functools.partial(_ffn_kernel, tm=tm, sub=sub, nj=nj)
    mrow = lambda k: pl.BlockSpec((None, 1, D_MODEL), lambda i, j: ((i // tiles_per_batch) * 6 + k, 0, 0))
    return pl.pallas_call(
        kern,
        out_shape=jax.ShapeDtypeStruct((t, D_MODEL), F32),
        grid=(t // tm, nj),
        in_specs=[pl.BlockSpec((tm, D_MODEL), lambda i, j: (i, 0)),
                  pl.BlockSpec((1, D_MODEL), lambda i, j: (0, 0)),
                  mrow(4), mrow(3), mrow(5),
                  pl.BlockSpec((D_MODEL, tf), lambda i, j: (0, j)),
                  pl.BlockSpec((D_MODEL, tf), lambda i, j: (0, nj + j)),
                  pl.BlockSpec((tf, D_MODEL), lambda i, j: (j, 0))],
        out_specs=pl.BlockSpec((tm, D_MODEL), lambda i, j: (i, 0)),
        scratch_shapes=[pltpu.VMEM((tm, D_MODEL), BF16), pltpu.VMEM((tm, D_MODEL), F32)],
        compiler_params=_cparams(("parallel", "arbitrary")),
        name="ffn",
    )(x1, nw, mod3, mod3, mod3, wgu_bf, wgu_bf, wd_bf)


def _rope_partner_index():
    idx = np.arange(LANES)
    d = idx % DA_DH
    half = ROPE_DIM // 2
    return np.where(d < half, idx + half, np.where(d < ROPE_DIM, idx - half, idx))


def kernel(x, c, positions, norm1_w, norm2_w, w_ada, b_ada, w_in, mlstm_conv_w, mlstm_conv_b, mlstm_gate_b,
           mlstm_norm_w, q_norm_w, k_norm_w, lambda_q1, lambda_k1, lambda_q2, lambda_k2, subln_w, w_out,
           w_gate_up, w_down):
    B, S, D = x.shape
    T = B * S
    depth = w_in.shape[0]
    xf = x.reshape(T, D)
    c8 = jnp.pad(c, ((0, 8 - B), (0, 0)))
    pos2 = positions.reshape(T, 1)

    inv_freq = ROPE_THETA ** (-jnp.arange(0, ROPE_DIM, 2, dtype=F32) / ROPE_DIM)
    lane = np.arange(LANES)
    dl = lane % DA_DH
    invf = jnp.where(jnp.asarray(dl < ROPE_DIM), inv_freq[dl % (ROPE_DIM // 2)], 0.0).reshape(1, LANES)
    pidx = _rope_partner_index()

    for layer in range(depth):
        lambda_init = 0.8 - 0.6 * math.exp(-0.3 * layer)
        mod = _ada(c8, w_ada[layer], b_ada[layer].reshape(1, -1))
        mod3 = mod[:B].reshape(B * 6, 1, D)

        w_l = w_in[layer]
        w_bf = jnp.concatenate([w_l[:, :GATE_LO], w_l[:, GATE_HI:]], axis=1).astype(BF16)
        wgate = jnp.pad(w_l[:, GATE_LO:GATE_HI], ((0, 0), (0, LANES - 2 * ML_HEADS)))
        wg_hi = wgate.astype(BF16)
        wg_lo = (wgate - wg_hi.astype(F32)).astype(BF16)
        wg2 = jnp.stack([wg_hi, wg_lo])

        p, g = _inproj(xf, norm1_w[layer].reshape(1, D), mod3, w_bf, wg2, S)

        gb = jnp.pad(mlstm_gate_b[layer].reshape(1, -1), ((0, 0), (0, LANES - 2 * ML_HEADS)))
        hml = _mlstm(p, g, mlstm_conv_w[layer], mlstm_conv_b[layer].reshape(1, -1), gb,
                     mlstm_norm_w[layer].reshape(1, ML_WIDTH), B, S)

        qw = jnp.tile(q_norm_w[layer], 2).reshape(1, LANES)
        kw = jnp.tile(k_norm_w[layer], 2).reshape(1, LANES)
        qn, kn = _qkprep(pos2, p, invf, qw, qw[:, pidx], kw, kw[:, pidx])

        r64 = lambda a: a[layer].reshape(1, DA_DH).astype(F32)
        oda = _attn(qn, kn, p, r64(lambda_q1), r64(lambda_k1), r64(lambda_q2), r64(lambda_k2),
                    subln_w[layer].reshape(1, DA_DV), B, S, lambda_init)

        x1 = _outproj(xf, mod3, hml, oda, w_out[layer].astype(BF16), S)
        xf = _ffn(x1, norm2_w[layer].reshape(1, D), mod3, w_gate_up[layer].astype(BF16),
                  w_down[layer].astype(BF16), S)
    return xf.reshape(B, S, D)
```

```python
import functools
import math

import jax
import jax.numpy as jnp
import numpy as np
from jax import lax
from jax.experimental import pallas as pl
from jax.experimental.pallas import tpu as pltpu

F32 = jnp.float32
BF16 = jnp.bfloat16

D_MODEL = 2048
CHUNK = 64
ML_HEADS = 4
ML_DQK = 128
ML_DV = 256
ML_CONV = 4
GATE_CAP = 15.0
DA_HEADS = 8
DA_DH = 64
DA_DV = 128
ROPE_DIM = 16
ROPE_THETA = 500000.0
ML_WIDTH = ML_HEADS * ML_DV
DA_WIDTH = DA_HEADS * DA_DV
QK_W = ML_HEADS * ML_DQK
D_FF = 5632
EPS = 1e-6
NEG = -1e30
LOG2E = 1.4426950408889634

LANES = 128
VMEM_LIMIT = 52 * 1024 * 1024

P_COLS = 6144
GATE_LO = 3 * 1024
GATE_HI = GATE_LO + 2 * ML_HEADS

ML_L = 256
AT_T = 512


def _cparams(sem):
    return pltpu.CompilerParams(dimension_semantics=sem, vmem_limit_bytes=VMEM_LIMIT)


def _dot(a, b):
    return jnp.dot(a, b, preferred_element_type=F32)


def _split3(x):
    hi = x.astype(BF16)
    r1 = x - hi.astype(F32)
    mid = r1.astype(BF16)
    lo = (r1 - mid.astype(F32)).astype(BF16)
    return hi, mid, lo


def _ada_kernel(c_ref, w_ref, b_ref, o_ref):
    c = c_ref[...]
    a = (c * jax.nn.sigmoid(c)).astype(BF16)
    o_ref[...] = _dot(a, w_ref[...].astype(BF16)) + b_ref[...]


def _ada(c8, w, b):
    n = w.shape[1]
    tn = 1024
    return pl.pallas_call(
        _ada_kernel,
        out_shape=jax.ShapeDtypeStruct((8, n), F32),
        grid=(n // tn,),
        in_specs=[pl.BlockSpec((8, D_MODEL), lambda j: (0, 0)),
                  pl.BlockSpec((D_MODEL, tn), lambda j: (0, j)),
                  pl.BlockSpec((1, tn), lambda j: (0, j))],
        out_specs=pl.BlockSpec((8, tn), lambda j: (0, j)),
        compiler_params=_cparams(("parallel",)),
        name="adaln",
    )(c8, w, b)


def _norm_mod_rows(x_ref, nw_ref, sc_ref, sh_ref, emit, tm, sub=128):
    gain = nw_ref[...] * (1.0 + sc_ref[...])
    sh = sh_ref[...]

    def body(r, carry):
        off = pl.multiple_of(r * sub, sub)
        x = x_ref[pl.ds(off, sub), :]
        ms = jnp.mean(x * x, axis=-1, keepdims=True)
        h = (x * lax.rsqrt(ms + EPS)) * gain + sh
        emit(off, h)
        return carry

    lax.fori_loop(0, tm // sub, body, 0)


def _inproj_kernel(x_ref, nw_ref, sc_ref, sh_ref, wa_ref, wb_ref, wg_ref, o_ref, g_ref, h_ref, *, tm, sub, na):
    j = pl.program_id(1)

    @pl.when(j == 0)
    def _():
        wg = wg_ref[...]
        def emit(off, h):
            hh = h.astype(BF16)
            h_ref[pl.ds(off, sub), :] = hh
            g_ref[pl.ds(off, sub), :] = _dot(hh, wg[0]) + _dot(hh, wg[1])
        _norm_mod_rows(x_ref, nw_ref, sc_ref, sh_ref, emit, tm, sub)

    @pl.when(j < na)
    def _():
        o_ref[...] = _dot(h_ref[...], wa_ref[...]).astype(BF16)

    @pl.when(j >= na)
    def _():
        o_ref[...] = _dot(h_ref[...], wb_ref[...]).astype(BF16)


def _inproj(x2, nw, mod3, w_a, w_b, wg2, tpb):
    t = x2.shape[0]
    tm, tn, sub = 1024, 1024, 128
    tiles_per_batch = tpb // tm
    na = w_a.shape[1] // tn
    kern = functools.partial(_inproj_kernel, tm=tm, sub=sub, na=na)
    return pl.pallas_call(
        kern,
        out_shape=(jax.ShapeDtypeStruct((t, P_COLS), BF16), jax.ShapeDtypeStruct((t, LANES), F32)),
        grid=(t // tm, P_COLS // tn),
        in_specs=[pl.BlockSpec((tm, D_MODEL), lambda i, j: (i, 0)),
                  pl.BlockSpec((1, D_MODEL), lambda i, j: (0, 0)),
                  pl.BlockSpec((None, 1, D_MODEL), lambda i, j: ((i // tiles_per_batch) * 6 + 1, 0, 0)),
                  pl.BlockSpec((None, 1, D_MODEL), lambda i, j: ((i // tiles_per_batch) * 6 + 0, 0, 0)),
                  pl.BlockSpec((D_MODEL, tn), lambda i, j: (0, jnp.minimum(j, na - 1))),
                  pl.BlockSpec((D_MODEL, tn), lambda i, j: (0, jnp.maximum(j - na, 0))),
                  pl.BlockSpec((2, D_MODEL, LANES), lambda i, j: (0, 0, 0))],
        out_specs=(pl.BlockSpec((tm, tn), lambda i, j: (i, j)),
                   pl.BlockSpec((tm, LANES), lambda i, j: (i, 0))),
        scratch_shapes=[pltpu.VMEM((tm, D_MODEL), BF16)],
        compiler_params=_cparams(("parallel", "arbitrary")),
        name="inproj",
    )(x2, nw, mod3, mod3, w_a, w_b, wg2)


def _conv_silu(x, tail8, w4, b, row8):
    acc = b + w4[3:4, :] * x
    for k in (1, 2, 3):
        xr = pltpu.roll(x, k, 0)
        tr = pltpu.roll(tail8, k, 0)
        first = jnp.where(row8 < k, tr, xr[0:8, :])
        xs = jnp.concatenate([first, xr[8:, :]], axis=0)
        acc = acc + w4[3 - k:4 - k, :] * xs
    return acc * jax.nn.sigmoid(acc)


def _mlstm_kernel(qk_ref, v_ref, og_ref, g_ref, cw_ref, cb_ref, gb_ref, nw_ref, o_ref, ct_ref, tail_ref):
    L = ML_L
    c = pl.program_id(1)

    @pl.when(c == 0)
    def _():
        ct_ref[...] = jnp.zeros_like(ct_ref)
        tail_ref[...] = jnp.zeros_like(tail_ref)

    g = g_ref[...] + gb_ref[...]
    gc = GATE_CAP * jnp.tanh(g * (1.0 / GATE_CAP))
    lf_all = -jnp.log1p(jnp.exp(-gc))
    ri = lax.broadcasted_iota(jnp.int32, (L, L), 0)
    ci = lax.broadcasted_iota(jnp.int32, (L, L), 1)
    tri = ci <= ri
    trib = jnp.where(tri, 1.0, 0.0).astype(BF16)
    f_hi, f_mid, f_lo = _split3(lf_all)
    b_all = _dot(trib, f_hi) + _dot(trib, f_mid) + _dot(trib, f_lo)
    u_all = gc - pltpu.roll(b_all, LANES - ML_HEADS, 1)
    u_t = u_all.T

    row8 = lax.broadcasted_iota(jnp.int32, (8, LANES), 0)
    ones_b = jnp.ones((L, LANES), BF16)
    tails = tail_ref[...]

    for h in range(ML_HEADS):
        qs = slice(h * ML_DQK, (h + 1) * ML_DQK)
        ks = slice(QK_W + h * ML_DQK, QK_W + (h + 1) * ML_DQK)
        vs = slice(h * ML_DV, (h + 1) * ML_DV)
        q = _conv_silu(qk_ref[:, qs].astype(F32), tails[:, qs], cw_ref[:, qs], cb_ref[:, qs], row8)
        k = _conv_silu(qk_ref[:, ks].astype(F32), tails[:, ks], cw_ref[:, ks], cb_ref[:, ks], row8)
        q = q * (ML_DQK ** -0.5)

        b_col = jnp.broadcast_to(b_all[:, ML_HEADS + h:ML_HEADS + h + 1], (L, LANES))
        u_col = jnp.broadcast_to(u_all[:, h:h + 1], (L, LANES))
        b_end = b_all[L - 1:L, ML_HEADS + h:ML_HEADS + h + 1]
        u_row = u_t[h:h + 1, :]

        logd = jnp.concatenate([b_col, b_col], axis=1) + u_row
        d = jnp.exp(jnp.where(tri, logd, NEG))
        s = lax.dot_general(q.astype(BF16), k.astype(BF16), (((1,), (1,)), ((), ())),
                            preferred_element_type=F32) * d

        vaug = jnp.concatenate([v_ref[:, vs], ones_b], axis=1)
        ct = ct_ref[h]
        tot = _dot((q * jnp.exp(b_col)).astype(BF16), ct.astype(BF16)) + _dot(s.astype(BF16), vaug)
        den = tot[:, ML_DV:]
        den2 = jnp.concatenate([den, den], axis=1)
        hc = tot[:, :ML_DV] / jnp.maximum(jnp.abs(den2), 1.0)

        kw = (k * jnp.exp(b_end + u_col)).astype(BF16)
        upd = lax.dot_general(kw, vaug, (((0,), (0,)), ((), ())), preferred_element_type=F32)
        ct_ref[h] = jnp.exp(b_end) * ct + upd

        ms = jnp.mean(hc * hc, axis=-1, keepdims=True)
        hn = hc * lax.rsqrt(ms + EPS) * nw_ref[:, vs]
        o_ref[:, vs] = (hn * jax.nn.sigmoid(og_ref[:, vs].astype(F32))).astype(BF16)

    tail_ref[...] = qk_ref[L - 16:L, :].astype(F32)[8:16, :]


def _mlstm(p, g, cw, cb, gb, nw, batch, seq):
    t = p.shape[0]
    L = ML_L
    nc = seq // L
    return pl.pallas_call(
        _mlstm_kernel,
        out_shape=jax.ShapeDtypeStruct((t, ML_WIDTH), BF16),
        grid=(batch, nc),
        in_specs=[pl.BlockSpec((L, 1024), lambda b, c: (b * nc + c, 0)),
                  pl.BlockSpec((L, 1024), lambda b, c: (b * nc + c, 1)),
                  pl.BlockSpec((L, 1024), lambda b, c: (b * nc + c, 2)),
                  pl.BlockSpec((L, LANES), lambda b, c: (b * nc + c, 0)),
                  pl.BlockSpec((ML_CONV, 1024), lambda b, c: (0, 0)),
                  pl.BlockSpec((1, 1024), lambda b, c: (0, 0)),
                  pl.BlockSpec((1, LANES), lambda b, c: (0, 0)),
                  pl.BlockSpec((1, ML_WIDTH), lambda b, c: (0, 0))],
        out_specs=pl.BlockSpec((L, ML_WIDTH), lambda b, c: (b * nc + c, 0)),
        scratch_shapes=[pltpu.VMEM((ML_HEADS, ML_DQK, ML_DV + LANES), F32),
                        pltpu.VMEM((8, 1024), F32)],
        compiler_params=_cparams(("parallel", "arbitrary")),
        name="mlstm",
    )(p, p, p, g, cw, cb, gb, nw)


def _qkprep_kernel(pos_ref, q_ref, k_ref, invf_ref, qw_ref, qpw_ref, kw_ref, kpw_ref, qo_ref, ko_ref):
    half = ROPE_DIM // 2
    tm = q_ref.shape[0]
    pos = pos_ref[0].astype(F32)
    ang = jnp.concatenate([invf_ref[...]] * (tm // LANES), axis=1) * pos
    fi = lax.broadcasted_iota(jnp.int32, (4 * half, LANES), 0)
    li = lax.broadcasted_iota(jnp.int32, (4 * half, LANES), 1)
    dl = li & (DA_DH - 1)
    hit = (dl < ROPE_DIM) & ((dl & (half - 1)) == (fi & (half - 1))) & (fi < 3 * half)
    e_cos = jnp.where(hit, 1.0, 0.0).astype(BF16)
    e_sin = jnp.where(hit, jnp.where(dl < half, -1.0, 1.0), 0.0).astype(BF16)

    def expand(t8, e):
        hi, mid, lo = _split3(t8)
        parts = jnp.concatenate([hi.astype(F32), mid.astype(F32), lo.astype(F32), jnp.zeros_like(t8)], axis=0)
        return lax.dot_general(parts.astype(BF16), e, (((0,), (0,)), ((), ())), preferred_element_type=F32)

    lane = lax.broadcasted_iota(jnp.int32, (1, LANES), 1)
    cs = expand(jnp.cos(ang), e_cos) + jnp.where((lane & (DA_DH - 1)) < ROPE_DIM, 0.0, 1.0)
    sn = expand(jnp.sin(ang), e_sin)
    ri = lax.broadcasted_iota(jnp.int32, (LANES, LANES), 0)
    ci = lax.broadcasted_iota(jnp.int32, (LANES, LANES), 1)
    dc = ci & (DA_DH - 1)
    partner = jnp.where(dc < half, ci + half, jnp.where(dc < ROPE_DIM, ci - half, -1))
    perm = jnp.where(ri == partner, 1.0, 0.0).astype(BF16)
    ones_blk = jnp.where((ri >> 6) == (ci >> 6), 1.0, 0.0).astype(BF16)

    def prep(x_ref, w_ref, pw_ref, o_ref, scale):
        a = (w_ref[...] * scale) * cs
        bc = (pw_ref[...] * scale) * sn
        for hh in range(DA_HEADS):
            cols = slice(hh * LANES, (hh + 1) * LANES)
            xb = x_ref[:, cols]
            xf = xb.astype(F32)
            sq = xf * xf
            sqh = sq.astype(BF16)
            sql = (sq - sqh.astype(F32)).astype(BF16)
            ms = (_dot(sqh, ones_blk) + _dot(sql, ones_blk)) * (1.0 / DA_DH)
            r = lax.rsqrt(ms + EPS)
            px = _dot(xb, perm)
            o_ref[:, cols] = (r * (xf * a + px * bc)).astype(BF16)

    prep(q_ref, qw_ref, qpw_ref, qo_ref, DA_DH ** -0.5 * LOG2E)
    prep(k_ref, kw_ref, kpw_ref, ko_ref, 1.0)


def _qkprep(positions, p, invf, qw, qpw, kw, kpw):
    t = p.shape[0]
    tm = 512
    pos3 = positions.reshape(t // tm, 1, tm)
    vec = pl.BlockSpec((1, LANES), lambda i: (0, 0))
    return pl.pallas_call(
        _qkprep_kernel,
        out_shape=(jax.ShapeDtypeStruct((t, 1024), BF16), jax.ShapeDtypeStruct((t, 1024), BF16)),
        grid=(t // tm,),
        in_specs=[pl.BlockSpec((1, 1, tm), lambda i: (i, 0, 0)),
                  pl.BlockSpec((tm, 1024), lambda i: (i, 3)),
                  pl.BlockSpec((tm, 1024), lambda i: (i, 4)),
                  pl.BlockSpec((ROPE_DIM // 2, LANES), lambda i: (0, 0)),
                  vec, vec, vec, vec],
        out_specs=(pl.BlockSpec((tm, 1024), lambda i: (i, 0)),
                   pl.BlockSpec((tm, 1024), lambda i: (i, 0))),
        compiler_params=_cparams(("parallel",)),
        name="qkprep",
    )(pos3, p, p, invf, qw, qpw, kw, kpw)


def _attn_kernel(q_ref, k_ref, v_ref, lq1_ref, lk1_ref, lq2_ref, lk2_ref, sw_ref, o_ref, m_ref, acc_ref,
                 sa_ref, sb_ref, *, lambda_init):
    T = AT_T
    qi = pl.program_id(2)
    q = q_ref[...]
    lane = lax.broadcasted_iota(jnp.int32, (T, LANES), 1)
    zero = jnp.zeros_like(q)
    qm = (jnp.where(lane < DA_DH, q, zero), jnp.where(lane >= DA_DH, q, zero))
    m_ref[...] = jnp.full(m_ref.shape, NEG, F32)
    acc_ref[...] = jnp.zeros(acc_ref.shape, F32)
    ones_b = jnp.ones((T, LANES), BF16)

    sbufs = (sa_ref, sb_ref)

    def score(j, buf, masked):
        off = pl.multiple_of(j * T, T)
        kb = k_ref[pl.ds(off, T), :]
        for mi in range(2):
            s = lax.dot_general(qm[mi], kb, (((1,), (1,)), ((), ())), preferred_element_type=F32)
            if masked:
                r = lax.broadcasted_iota(jnp.int32, (T, T), 0)
                cc = lax.broadcasted_iota(jnp.int32, (T, T), 1)
                s = jnp.where((cc >> 6) <= (r >> 6), s, NEG)
            sbufs[buf][mi] = s

    def consume(j, buf):
        off = pl.multiple_of(j * T, T)
        vaug = jnp.concatenate([v_ref[pl.ds(off, T), :], ones_b], axis=1)
        for mi in range(2):
            s = sbufs[buf][mi]
            m_old = m_ref[mi]
            m_new = jnp.maximum(m_old, jnp.max(s, axis=1, keepdims=True))
            al = jnp.exp2(m_old - m_new)
            p = jnp.exp2(s - jnp.concatenate([m_new] * (T // LANES), axis=1)).astype(BF16)
            acc_ref[mi] = acc_ref[mi] * jnp.concatenate([al, al], axis=1) + _dot(p, vaug)
            m_ref[mi] = m_new

    score(qi, 0, True)

    def pair(i, carry):
        score(2 * i, 1, False)
        consume(jnp.where(i == 0, qi, 2 * i - 1), 0)
        score(2 * i + 1, 0, False)
        consume(2 * i, 1)
        return carry

    npair = qi // 2
    lax.fori_loop(0, npair, pair, 0)
    last0 = jnp.where(npair == 0, qi, 2 * npair - 1)

    @pl.when(qi % 2 == 1)
    def _():
        score(qi - 1, 1, False)
        consume(last0, 0)
        consume(qi - 1, 1)

    @pl.when(qi % 2 == 0)
    def _():
        consume(last0, 0)

    acc0 = acc_ref[0]
    acc1 = acc_ref[1]
    o0 = acc0[:, :DA_DV] / acc0[:, DA_DV:]
    o1 = acc1[:, :DA_DV] / acc1[:, DA_DV:]
    lam = (jnp.exp(jnp.sum(lq1_ref[...] * lk1_ref[...], axis=-1, keepdims=True))
           - jnp.exp(jnp.sum(lq2_ref[...] * lk2_ref[...], axis=-1, keepdims=True)) + lambda_init)
    o = o0 - lam * o1
    ms = jnp.mean(o * o, axis=-1, keepdims=True)
    o_ref[...] = (o * lax.rsqrt(ms + EPS) * (sw_ref[...] * (1.0 - lambda_init))).astype(BF16)


def _attn(qn, kn, p, lq1, lk1, lq2, lk2, sw, batch, seq, lambda_init):
    t = qn.shape[0]
    T = AT_T
    nq = seq // T
    vcol0 = 5 * 1024 // LANES
    vec64 = pl.BlockSpec((1, DA_DH), lambda b, h, i: (0, 0))
    kern = functools.partial(_attn_kernel, lambda_init=lambda_init)
    return pl.pallas_call(
        kern,
        out_shape=jax.ShapeDtypeStruct((t, DA_WIDTH), BF16),
        grid=(batch, DA_HEADS, nq),
        in_specs=[pl.BlockSpec((T, LANES), lambda b, h, i: (b * nq + i, h)),
                  pl.BlockSpec((seq, LANES), lambda b, h, i: (b, h)),
                  pl.BlockSpec((seq, LANES), lambda b, h, i: (b, vcol0 + h)),
                  vec64, vec64, vec64, vec64,
                  pl.BlockSpec((1, DA_DV), lambda b, h, i: (0, 0))],
        out_specs=pl.BlockSpec((T, LANES), lambda b, h, i: (b * nq + i, h)),
        scratch_shapes=[pltpu.VMEM((2, T, LANES), F32), pltpu.VMEM((2, T, 2 * LANES), F32),
                        pltpu.VMEM((2, T, T), F32), pltpu.VMEM((2, T, T), F32)],
        compiler_params=_cparams(("parallel", "parallel", "arbitrary")),
        name="diffattn",
    )(qn, kn, p, lq1, lk1, lq2, lk2, sw)


def _outproj_kernel(x_ref, g_ref, a_ref, b_ref, wa_ref, wb_ref, o_ref):
    mix = _dot(a_ref[...], wa_ref[...]) + _dot(b_ref[...], wb_ref[...])
    o_ref[...] = x_ref[...] + g_ref[...] * mix


def _outproj(x2, mod3, hml, oda, w_bf, tpb):
    t = x2.shape[0]
    tm, tn = 1024, 1024
    tiles_per_batch = tpb // tm
    nb = D_MODEL // tn
    return pl.pallas_call(
        _outproj_kernel,
        out_shape=jax.ShapeDtypeStruct((t, D_MODEL), F32),
        grid=(t // tm, nb),
        in_specs=[pl.BlockSpec((tm, tn), lambda i, j: (i, j)),
                  pl.BlockSpec((None, 1, tn), lambda i, j: ((i // tiles_per_batch) * 6 * nb + 2 * nb + j, 0, 0)),
                  pl.BlockSpec((tm, ML_WIDTH), lambda i, j: (i, 0)),
                  pl.BlockSpec((tm, DA_WIDTH), lambda i, j: (i, 0)),
                  pl.BlockSpec((ML_WIDTH, tn), lambda i, j: (0, j)),
                  pl.BlockSpec((DA_WIDTH, tn), lambda i, j: (1, j))],
        out_specs=pl.BlockSpec((tm, tn), lambda i, j: (i, j)),
        compiler_params=_cparams(("parallel", "parallel")),
        name="outproj",
    )(x2, mod3.reshape(-1, 1, tn), hml, oda, w_bf, w_bf)


def _ffn_kernel(x_ref, nw_ref, sc_ref, sh_ref, gt_ref, wg_ref, wu_ref, wd_ref, o_ref, h_ref, acc_ref,
                *, tm, sub, nj):
    j = pl.program_id(1)

    @pl.when(j == 0)
    def _():
        def emit(off, h):
            h_ref[pl.ds(off, sub), :] = h.astype(BF16)
        _norm_mod_rows(x_ref, nw_ref, sc_ref, sh_ref, emit, tm, sub)
        acc_ref[...] = jnp.zeros_like(acc_ref)

    h = h_ref[...]
    g = _dot(h, wg_ref[...])
    u = _dot(h, wu_ref[...])
    a = (g * jax.nn.sigmoid(g) * u).astype(BF16)
    acc_ref[...] += _dot(a, wd_ref[...])

    @pl.when(j == nj - 1)
    def _():
        o_ref[...] = x_ref[...] + gt_ref[...] * acc_ref[...]


def _ffn(x1, nw, mod3, wgu_bf, wd_bf, tpb):
    t = x1.shape[0]
    tm, tf, sub = 512, 512, 128
    tiles_per_batch = tpb // tm
    nj = D_FF // tf
    kern = functools.partial(_ffn_kernel, tm=tm, sub=sub, nj=nj)
    mrow = lambda k: pl.BlockSpec((None, 1, D_MODEL), lambda i, j: ((i // tiles_per_batch) * 6 + k, 0, 0))
    return pl.pallas_call(
        kern,
        out_shape=jax.ShapeDtypeStruct((t, D_MODEL), F32),
        grid=(t // tm, nj),
        in_specs=[pl.BlockSpec((tm, D_MODEL), lambda i, j: (i, 0)),
                  pl.BlockSpec((1, D_MODEL), lambda i, j: (0, 0)),
                  mrow(4), mrow(3), mrow(5),
                  pl.BlockSpec((D_MODEL, tf), lambda i, j: (0, j)),
                  pl.BlockSpec((D_MODEL, tf), lambda i, j: (0, nj + j)),
                  pl.BlockSpec((tf, D_MODEL), lambda i, j: (j, 0))],
        out_specs=pl.BlockSpec((tm, D_MODEL), lambda i, j: (i, 0)),
        scratch_shapes=[pltpu.VMEM((tm, D_MODEL), BF16), pltpu.VMEM((tm, D_MODEL), F32)],
        compiler_params=_cparams(("parallel", "arbitrary")),
        name="ffn",
    )(x1, nw, mod3, mod3, mod3, wgu_bf, wgu_bf, wd_bf)


def _rope_partner_index():
    idx = np.arange(LANES)
    d = idx % DA_DH
    half = ROPE_DIM // 2
    return np.where(d < half, idx + half, np.where(d < ROPE_DIM, idx - half, idx))


def kernel(x, c, positions, norm1_w, norm2_w, w_ada, b_ada, w_in, mlstm_conv_w, mlstm_conv_b, mlstm_gate_b,
           mlstm_norm_w, q_norm_w, k_norm_w, lambda_q1, lambda_k1, lambda_q2, lambda_k2, subln_w, w_out,
           w_gate_up, w_down):
    B, S, D = x.shape
    T = B * S
    depth = w_in.shape[0]
    xf = x.reshape(T, D)
    c8 = jnp.pad(c, ((0, 8 - B), (0, 0)))

    inv_freq = ROPE_THETA ** (-jnp.arange(0, ROPE_DIM, 2, dtype=F32) / ROPE_DIM)
    invf = jnp.broadcast_to(inv_freq[:, None], (ROPE_DIM // 2, LANES))
    pidx = _rope_partner_index()

    for layer in range(depth):
        lambda_init = 0.8 - 0.6 * math.exp(-0.3 * layer)
        mod = _ada(c8, w_ada[layer], b_ada[layer].reshape(1, -1))
        mod3 = mod[:B].reshape(B * 6, 1, D)

        w_l = w_in[layer]
        w_a = w_l[:, :GATE_LO].astype(BF16)
        w_b = w_l[:, GATE_HI:].astype(BF16)
        wgate = jnp.pad(w_l[:, GATE_LO:GATE_HI], ((0, 0), (0, LANES - 2 * ML_HEADS)))
        wg_hi = wgate.astype(BF16)
        wg_lo = (wgate - wg_hi.astype(F32)).astype(BF16)
        wg2 = jnp.stack([wg_hi, wg_lo])

        p, g = _inproj(xf, norm1_w[layer].reshape(1, D), mod3, w_a, w_b, wg2, S)

        gb = jnp.pad(mlstm_gate_b[layer].reshape(1, -1), ((0, 0), (0, LANES - 2 * ML_HEADS)))
        hml = _mlstm(p, g, mlstm_conv_w[layer], mlstm_conv_b[layer].reshape(1, -1), gb,
                     mlstm_norm_w[layer].reshape(1, ML_WIDTH), B, S)

        qw = jnp.tile(q_norm_w[layer], 2).reshape(1, LANES)
        kw = jnp.tile(k_norm_w[layer], 2).reshape(1, LANES)
        qn, kn = _qkprep(positions, p, invf, qw, qw[:, pidx], kw, kw[:, pidx])

        r64 = lambda a: a[layer].reshape(1, DA_DH).astype(F32)
        oda = _attn(qn, kn, p, r64(lambda_q1), r64(lambda_k1), r64(lambda_q2), r64(lambda_k2),
                    subln_w[layer].reshape(1, DA_DV), B, S, lambda_init)

        x1 = _outproj(xf, mod3, hml, oda, w_out[layer].astype(BF16), S)
        xf = _ffn(x1, norm2_w[layer].reshape(1, D), mod3, w_gate_up[layer].astype(BF16),
                  w_down[layer].astype(BF16), S)
    return xf.reshape(B, S, D)
```

```python
import functools
import math

import jax
import jax.numpy as jnp
import numpy as np
from jax import lax
from jax.experimental import pallas as pl
from jax.experimental.pallas import tpu as pltpu

F32 = jnp.float32
BF16 = jnp.bfloat16

D_MODEL = 2048
CHUNK = 64
ML_HEADS = 4
ML_DQK = 128
ML_DV = 256
ML_CONV = 4
GATE_CAP = 15.0
DA_HEADS = 8
DA_DH = 64
DA_DV = 128
ROPE_DIM = 16
ROPE_THETA = 500000.0
ML_WIDTH = ML_HEADS * ML_DV
DA_WIDTH = DA_HEADS * DA_DV
QK_W = ML_HEADS * ML_DQK
D_FF = 5632
EPS = 1e-6
NEG = -1e30
LOG2E = 1.4426950408889634

LANES = 128
VMEM_LIMIT = 52 * 1024 * 1024
VMEM_LIMIT_BIG = 60 * 1024 * 1024

P_COLS = 6144
GATE_LO = 3 * 1024
GATE_HI = GATE_LO + 2 * ML_HEADS

ML_L = 256
AT_T = 512


def _cparams(sem, vmem=VMEM_LIMIT):
    return pltpu.CompilerParams(dimension_semantics=sem, vmem_limit_bytes=vmem)


def _dot(a, b):
    return jnp.dot(a, b, preferred_element_type=F32)


def _dot_nt(a, b):
    return lax.dot_general(a, b, (((1,), (1,)), ((), ())), preferred_element_type=F32)


def _split3(x):
    hi = x.astype(BF16)
    r1 = x - hi.astype(F32)
    mid = r1.astype(BF16)
    lo = (r1 - mid.astype(F32)).astype(BF16)
    return hi, mid, lo


def _ada_kernel(c_ref, w_ref, b_ref, o_ref):
    c = c_ref[...]
    a = (c * jax.nn.sigmoid(c)).astype(BF16)
    o_ref[...] = _dot(a, w_ref[...].astype(BF16)) + b_ref[...]


def _ada(c8, w, b):
    n = w.shape[1]
    tn = 1024
    return pl.pallas_call(
        _ada_kernel,
        out_shape=jax.ShapeDtypeStruct((8, n), F32),
        grid=(n // tn,),
        in_specs=[pl.BlockSpec((8, D_MODEL), lambda j: (0, 0)),
                  pl.BlockSpec((D_MODEL, tn), lambda j: (0, j)),
                  pl.BlockSpec((1, tn), lambda j: (0, j))],
        out_specs=pl.BlockSpec((8, tn), lambda j: (0, j)),
        compiler_params=_cparams(("parallel",)),
        name="adaln",
    )(c8, w, b)


def _norm_mod_rows(x_ref, nw_ref, sc_ref, sh_ref, emit, tm, sub=128):
    gain = nw_ref[...] * (1.0 + sc_ref[...])
    sh = sh_ref[...]

    def body(r, carry):
        off = pl.multiple_of(r * sub, sub)
        x = x_ref[pl.ds(off, sub), :]
        ms = jnp.mean(x * x, axis=-1, keepdims=True)
        h = (x * lax.rsqrt(ms + EPS)) * gain + sh
        emit(off, h)
        return carry

    lax.fori_loop(0, tm // sub, body, 0)


def _inproj_kernel(x_ref, nw_ref, sc_ref, sh_ref, wa_ref, wb_ref, wg_ref, o_ref, g_ref, h_ref, *, tm, sub, na):
    j = pl.program_id(1)

    @pl.when(j == 0)
    def _():
        wg = wg_ref[...]
        def emit(off, h):
            hh = h.astype(BF16)
            h_ref[pl.ds(off, sub), :] = hh
            g_ref[pl.ds(off, sub), :] = _dot_nt(hh, wg[0]) + _dot_nt(hh, wg[1])
        _norm_mod_rows(x_ref, nw_ref, sc_ref, sh_ref, emit, tm, sub)

    @pl.when(j < na)
    def _():
        o_ref[...] = _dot_nt(h_ref[...], wa_ref[...]).astype(BF16)

    @pl.when(j >= na)
    def _():
        o_ref[...] = _dot_nt(h_ref[...], wb_ref[...]).astype(BF16)


def _inproj(x2, nw, mod3, w_a, w_b, wg2, tpb):
    t = x2.shape[0]
    tm, tn, sub = 1024, 1024, 128
    tiles_per_batch = tpb // tm
    na = w_a.shape[0] // tn
    kern = functools.partial(_inproj_kernel, tm=tm, sub=sub, na=na)
    return pl.pallas_call(
        kern,
        out_shape=(jax.ShapeDtypeStruct((t, P_COLS), BF16), jax.ShapeDtypeStruct((t, LANES), F32)),
        grid=(t // tm, P_COLS // tn),
        in_specs=[pl.BlockSpec((tm, D_MODEL), lambda i, j: (i, 0)),
                  pl.BlockSpec((1, D_MODEL), lambda i, j: (0, 0)),
                  pl.BlockSpec((None, 1, D_MODEL), lambda i, j: ((i // tiles_per_batch) * 6 + 1, 0, 0)),
                  pl.BlockSpec((None, 1, D_MODEL), lambda i, j: ((i // tiles_per_batch) * 6 + 0, 0, 0)),
                  pl.BlockSpec((tn, D_MODEL), lambda i, j: (jnp.minimum(j, na - 1), 0)),
                  pl.BlockSpec((tn, D_MODEL), lambda i, j: (jnp.maximum(j - na, 0), 0)),
                  pl.BlockSpec((2, LANES, D_MODEL), lambda i, j: (0, 0, 0))],
        out_specs=(pl.BlockSpec((tm, tn), lambda i, j: (i, j)),
                   pl.BlockSpec((tm, LANES), lambda i, j: (i, 0))),
        scratch_shapes=[pltpu.VMEM((tm, D_MODEL), BF16)],
        compiler_params=_cparams(("parallel", "arbitrary")),
        name="inproj",
    )(x2, nw, mod3, mod3, w_a, w_b, wg2)


def _conv_silu(x, tail8, w4, b, row8):
    acc = b + w4[3:4, :] * x
    for k in (1, 2, 3):
        xr = pltpu.roll(x, k, 0)
        tr = pltpu.roll(tail8, k, 0)
        first = jnp.where(row8 < k, tr, xr[0:8, :])
        xs = jnp.concatenate([first, xr[8:, :]], axis=0)
        acc = acc + w4[3 - k:4 - k, :] * xs
    return acc * jax.nn.sigmoid(acc)


def _mlstm_kernel(qk_ref, v_ref, og_ref, g_ref, cw_ref, cb_ref, gb_ref, nw_ref, o_ref, ct_ref, tail_ref):
    L = ML_L
    c = pl.program_id(1)

    @pl.when(c == 0)
    def _():
        ct_ref[...] = jnp.zeros_like(ct_ref)
        tail_ref[...] = jnp.zeros_like(tail_ref)

    g = g_ref[...] + gb_ref[...]
    gc = GATE_CAP * jnp.tanh(g * (1.0 / GATE_CAP))
    lf_all = -jnp.log1p(jnp.exp(-gc))
    ri = lax.broadcasted_iota(jnp.int32, (L, L), 0)
    ci = lax.broadcasted_iota(jnp.int32, (L, L), 1)
    tri = ci <= ri
    trib = jnp.where(tri, 1.0, 0.0).astype(BF16)
    f_hi, f_mid, f_lo = _split3(lf_all)
    b_all = _dot(trib, f_hi) + _dot(trib, f_mid) + _dot(trib, f_lo)
    u_all = gc - pltpu.roll(b_all, LANES - ML_HEADS, 1)
    u_t = u_all.T

    row8 = lax.broadcasted_iota(jnp.int32, (8, LANES), 0)
    ones_b = jnp.ones((L, LANES), BF16)
    tails = tail_ref[...]

    for h in range(ML_HEADS):
        qs = slice(h * ML_DQK, (h + 1) * ML_DQK)
        ks = slice(QK_W + h * ML_DQK, QK_W + (h + 1) * ML_DQK)
        vs = slice(h * ML_DV, (h + 1) * ML_DV)
        q = _conv_silu(qk_ref[:, qs].astype(F32), tails[:, qs], cw_ref[:, qs], cb_ref[:, qs], row8)
        k = _conv_silu(qk_ref[:, ks].astype(F32), tails[:, ks], cw_ref[:, ks], cb_ref[:, ks], row8)
        q = q * (ML_DQK ** -0.5)

        b_col = jnp.broadcast_to(b_all[:, ML_HEADS + h:ML_HEADS + h + 1], (L, LANES))
        u_col = jnp.broadcast_to(u_all[:, h:h + 1], (L, LANES))
        b_end = b_all[L - 1:L, ML_HEADS + h:ML_HEADS + h + 1]
        u_row = u_t[h:h + 1, :]

        logd = jnp.concatenate([b_col, b_col], axis=1) + u_row
        d = jnp.exp(jnp.where(tri, logd, NEG))
        s = lax.dot_general(q.astype(BF16), k.astype(BF16), (((1,), (1,)), ((), ())),
                            preferred_element_type=F32) * d

        vaug = jnp.concatenate([v_ref[:, vs], ones_b], axis=1)
        ct = ct_ref[h]
        tot = _dot((q * jnp.exp(b_col)).astype(BF16), ct.astype(BF16)) + _dot(s.astype(BF16), vaug)
        den = tot[:, ML_DV:]
        den2 = jnp.concatenate([den, den], axis=1)
        hc = tot[:, :ML_DV] / jnp.maximum(jnp.abs(den2), 1.0)

        kw = (k * jnp.exp(b_end + u_col)).astype(BF16)
        upd = lax.dot_general(kw, vaug, (((0,), (0,)), ((), ())), preferred_element_type=F32)
        ct_ref[h] = jnp.exp(b_end) * ct + upd

        ms = jnp.mean(hc * hc, axis=-1, keepdims=True)
        hn = hc * lax.rsqrt(ms + EPS) * nw_ref[:, vs]
        o_ref[:, vs] = (hn * jax.nn.sigmoid(og_ref[:, vs].astype(F32))).astype(BF16)

    tail_ref[...] = qk_ref[L - 16:L, :].astype(F32)[8:16, :]


def _mlstm(p, g, cw, cb, gb, nw, batch, seq):
    t = p.shape[0]
    L = ML_L
    nc = seq // L
    return pl.pallas_call(
        _mlstm_kernel,
        out_shape=jax.ShapeDtypeStruct((t, ML_WIDTH), BF16),
        grid=(batch, nc),
        in_specs=[pl.BlockSpec((L, 1024), lambda b, c: (b * nc + c, 0)),
                  pl.BlockSpec((L, 1024), lambda b, c: (b * nc + c, 1)),
                  pl.BlockSpec((L, 1024), lambda b, c: (b * nc + c, 2)),
                  pl.BlockSpec((L, LANES), lambda b, c: (b * nc + c, 0)),
                  pl.BlockSpec((ML_CONV, 1024), lambda b, c: (0, 0)),
                  pl.BlockSpec((1, 1024), lambda b, c: (0, 0)),
                  pl.BlockSpec((1, LANES), lambda b, c: (0, 0)),
                  pl.BlockSpec((1, ML_WIDTH), lambda b, c: (0, 0))],
        out_specs=pl.BlockSpec((L, ML_WIDTH), lambda b, c: (b * nc + c, 0)),
        scratch_shapes=[pltpu.VMEM((ML_HEADS, ML_DQK, ML_DV + LANES), F32),
                        pltpu.VMEM((8, 1024), F32)],
        compiler_params=_cparams(("parallel", "arbitrary")),
        name="mlstm",
    )(p, p, p, g, cw, cb, gb, nw)


def _qkprep_kernel(pos_ref, q_ref, k_ref, invf_ref, qw_ref, qpw_ref, kw_ref, kpw_ref, qo_ref, ko_ref):
    half = ROPE_DIM // 2
    tm = q_ref.shape[0]
    pos = pos_ref[0].astype(F32)
    ang = jnp.concatenate([invf_ref[...]] * (tm // LANES), axis=1) * pos
    fi = lax.broadcasted_iota(jnp.int32, (4 * half, LANES), 0)
    li = lax.broadcasted_iota(jnp.int32, (4 * half, LANES), 1)
    dl = li & (DA_DH - 1)
    hit = (dl < ROPE_DIM) & ((dl & (half - 1)) == (fi & (half - 1))) & (fi < 3 * half)
    e_cos = jnp.where(hit, 1.0, 0.0).astype(BF16)
    e_sin = jnp.where(hit, jnp.where(dl < half, -1.0, 1.0), 0.0).astype(BF16)

    def expand(t8, e):
        hi, mid, lo = _split3(t8)
        parts = jnp.concatenate([hi.astype(F32), mid.astype(F32), lo.astype(F32), jnp.zeros_like(t8)], axis=0)
        return lax.dot_general(parts.astype(BF16), e, (((0,), (0,)), ((), ())), preferred_element_type=F32)

    lane = lax.broadcasted_iota(jnp.int32, (1, LANES), 1)
    cs = expand(jnp.cos(ang), e_cos) + jnp.where((lane & (DA_DH - 1)) < ROPE_DIM, 0.0, 1.0)
    sn = expand(jnp.sin(ang), e_sin)
    ri = lax.broadcasted_iota(jnp.int32, (LANES, LANES), 0)
    ci = lax.broadcasted_iota(jnp.int32, (LANES, LANES), 1)
    dc = ci & (DA_DH - 1)
    partner = jnp.where(dc < half, ci + half, jnp.where(dc < ROPE_DIM, ci - half, -1))
    perm = jnp.where(ri == partner, 1.0, 0.0).astype(BF16)
    mean_blk = jnp.where((ri >> 6) == (ci >> 6), 1.0 / DA_DH, 0.0).astype(BF16)

    def prep(x_ref, w_ref, pw_ref, o_ref, scale):
        a = (w_ref[...] * scale) * cs
        bc = (pw_ref[...] * scale) * sn
        for hh in range(DA_HEADS):
            cols = slice(hh * LANES, (hh + 1) * LANES)
            xb = x_ref[:, cols]
            xf = xb.astype(F32)
            r = lax.rsqrt(_dot((xf * xf).astype(BF16), mean_blk) + EPS)
            px = _dot(xb, perm)
            o_ref[:, cols] = (r * (xf * a + px * bc)).astype(BF16)

    prep(q_ref, qw_ref, qpw_ref, qo_ref, DA_DH ** -0.5 * LOG2E)
    prep(k_ref, kw_ref, kpw_ref, ko_ref, 1.0)


def _qkprep(positions, p, invf, qw, qpw, kw, kpw):
    t = p.shape[0]
    tm = 512
    pos3 = positions.reshape(t // tm, 1, tm)
    vec = pl.BlockSpec((1, LANES), lambda i: (0, 0))
    return pl.pallas_call(
        _qkprep_kernel,
        out_shape=(jax.ShapeDtypeStruct((t, 1024), BF16), jax.ShapeDtypeStruct((t, 1024), BF16)),
        grid=(t // tm,),
        in_specs=[pl.BlockSpec((1, 1, tm), lambda i: (i, 0, 0)),
                  pl.BlockSpec((tm, 1024), lambda i: (i, 3)),
                  pl.BlockSpec((tm, 1024), lambda i: (i, 4)),
                  pl.BlockSpec((ROPE_DIM // 2, LANES), lambda i: (0, 0)),
                  vec, vec, vec, vec],
        out_specs=(pl.BlockSpec((tm, 1024), lambda i: (i, 0)),
                   pl.BlockSpec((tm, 1024), lambda i: (i, 0))),
        compiler_params=_cparams(("parallel",)),
        name="qkprep",
    )(pos3, p, p, invf, qw, qpw, kw, kpw)


def _attn_kernel(q_ref, k_ref, v_ref, lq1_ref, lk1_ref, lq2_ref, lk2_ref, sw_ref, o_ref, m_ref, acc_ref,
                 sa_ref, sb_ref, *, lambda_init):
    T = AT_T
    qi = pl.program_id(2)
    q = q_ref[...]
    lane = lax.broadcasted_iota(jnp.int32, (T, LANES), 1)
    zero = jnp.zeros_like(q)
    qm = (jnp.where(lane < DA_DH, q, zero), jnp.where(lane >= DA_DH, q, zero))
    m_ref[...] = jnp.full(m_ref.shape, NEG, F32)
    acc_ref[...] = jnp.zeros(acc_ref.shape, F32)
    ones_b = jnp.ones((T, LANES), BF16)

    sbufs = (sa_ref, sb_ref)

    def score(j, buf, masked):
        off = pl.multiple_of(j * T, T)
        kb = k_ref[pl.ds(off, T), :]
        for mi in range(2):
            s = lax.dot_general(qm[mi], kb, (((1,), (1,)), ((), ())), preferred_element_type=F32)
            if masked:
                r = lax.broadcasted_iota(jnp.int32, (T, T), 0)
                cc = lax.broadcasted_iota(jnp.int32, (T, T), 1)
                s = jnp.where((cc >> 6) <= (r >> 6), s, NEG)
            sbufs[buf][mi] = s

    def consume(j, buf):
        off = pl.multiple_of(j * T, T)
        vaug = jnp.concatenate([v_ref[pl.ds(off, T), :], ones_b], axis=1)
        for mi in range(2):
            s = sbufs[buf][mi]
            m_old = m_ref[mi]
            m_new = jnp.maximum(m_old, jnp.max(s, axis=1, keepdims=True))
            al = jnp.exp2(m_old - m_new)
            p = jnp.exp2(s - jnp.concatenate([m_new] * (T // LANES), axis=1)).astype(BF16)
            acc_ref[mi] = acc_ref[mi] * jnp.concatenate([al, al], axis=1) + _dot(p, vaug)
            m_ref[mi] = m_new

    score(qi, 0, True)

    def pair(i, carry):
        score(2 * i, 1, False)
        consume(jnp.where(i == 0, qi, 2 * i - 1), 0)
        score(2 * i + 1, 0, False)
        consume(2 * i, 1)
        return carry

    npair = qi // 2
    lax.fori_loop(0, npair, pair, 0)
    last0 = jnp.where(npair == 0, qi, 2 * npair - 1)

    @pl.when(qi % 2 == 1)
    def _():
        score(qi - 1, 1, False)
        consume(last0, 0)
        consume(qi - 1, 1)

    @pl.when(qi % 2 == 0)
    def _():
        consume(last0, 0)

    acc0 = acc_ref[0]
    acc1 = acc_ref[1]
    o0 = acc0[:, :DA_DV] / acc0[:, DA_DV:]
    o1 = acc1[:, :DA_DV] / acc1[:, DA_DV:]
    lam = (jnp.exp(jnp.sum(lq1_ref[...] * lk1_ref[...], axis=-1, keepdims=True))
           - jnp.exp(jnp.sum(lq2_ref[...] * lk2_ref[...], axis=-1, keepdims=True)) + lambda_init)
    o = o0 - lam * o1
    ms = jnp.mean(o * o, axis=-1, keepdims=True)
    o_ref[...] = (o * lax.rsqrt(ms + EPS) * (sw_ref[...] * (1.0 - lambda_init))).astype(BF16)


def _attn(qn, kn, p, lq1, lk1, lq2, lk2, sw, batch, seq, lambda_init):
    t = qn.shape[0]
    T = AT_T
    nq = seq // T
    vcol0 = 5 * 1024 // LANES
    vec64 = pl.BlockSpec((1, DA_DH), lambda b, h, i: (0, 0))
    kern = functools.partial(_attn_kernel, lambda_init=lambda_init)
    return pl.pallas_call(
        kern,
        out_shape=jax.ShapeDtypeStruct((t, DA_WIDTH), BF16),
        grid=(batch, DA_HEADS, nq),
        in_specs=[pl.BlockSpec((T, LANES), lambda b, h, i: (b * nq + i, h)),
                  pl.BlockSpec((seq, LANES), lambda b, h, i: (b, h)),
                  pl.BlockSpec((seq, LANES), lambda b, h, i: (b, vcol0 + h)),
                  vec64, vec64, vec64, vec64,
                  pl.BlockSpec((1, DA_DV), lambda b, h, i: (0, 0))],
        out_specs=pl.BlockSpec((T, LANES), lambda b, h, i: (b * nq + i, h)),
        scratch_shapes=[pltpu.VMEM((2, T, LANES), F32), pltpu.VMEM((2, T, 2 * LANES), F32),
                        pltpu.VMEM((2, T, T), F32), pltpu.VMEM((2, T, T), F32)],
        compiler_params=_cparams(("parallel", "parallel", "arbitrary")),
        name="diffattn",
    )(qn, kn, p, lq1, lk1, lq2, lk2, sw)


def _outproj_kernel(x_ref, g_ref, a_ref, b_ref, wa_ref, wb_ref, o_ref):
    mix = _dot(a_ref[...], wa_ref[...]) + _dot(b_ref[...], wb_ref[...])
    o_ref[...] = x_ref[...] + g_ref[...] * mix


def _outproj(x2, mod3, hml, oda, w_bf, tpb):
    t = x2.shape[0]
    tm, tn = 1024, 1024
    tiles_per_batch = tpb // tm
    nb = D_MODEL // tn
    return pl.pallas_call(
        _outproj_kernel,
        out_shape=jax.ShapeDtypeStruct((t, D_MODEL), F32),
        grid=(t // tm, nb),
        in_specs=[pl.BlockSpec((tm, tn), lambda i, j: (i, j)),
                  pl.BlockSpec((None, 1, tn), lambda i, j: ((i // tiles_per_batch) * 6 * nb + 2 * nb + j, 0, 0)),
                  pl.BlockSpec((tm, ML_WIDTH), lambda i, j: (i, 0)),
                  pl.BlockSpec((tm, DA_WIDTH), lambda i, j: (i, 0)),
                  pl.BlockSpec((ML_WIDTH, tn), lambda i, j: (0, j)),
                  pl.BlockSpec((DA_WIDTH, tn), lambda i, j: (1, j))],
        out_specs=pl.BlockSpec((tm, tn), lambda i, j: (i, j)),
        compiler_params=_cparams(("parallel", "parallel")),
        name="outproj",
    )(x2, mod3.reshape(-1, 1, tn), hml, oda, w_bf, w_bf)


def _ffn_kernel(x_ref, nw_ref, sc_ref, sh_ref, gt_ref, wg_ref, wu_ref, wd_ref, o_ref, h_ref, *, tm, sub, nj):
    j = pl.program_id(1)

    @pl.when(j == 0)
    def _():
        def emit(off, h):
            h_ref[pl.ds(off, sub), :] = h.astype(BF16)
        _norm_mod_rows(x_ref, nw_ref, sc_ref, sh_ref, emit, tm, sub)
        o_ref[...] = jnp.zeros_like(o_ref)

    h = h_ref[...]
    g = _dot(h, wg_ref[...])
    u = _dot(h, wu_ref[...])
    a = (g * jax.nn.sigmoid(g) * u).astype(BF16)
    o_ref[...] += _dot(a, wd_ref[...])

    @pl.when(j == nj - 1)
    def _():
        o_ref[...] = x_ref[...] + gt_ref[...] * o_ref[...]


def _ffn(x1, nw, mod3, wgu_bf, wd_bf, tpb):
    t = x1.shape[0]
    tm, tf, sub = 1024, 512, 128
    tiles_per_batch = tpb // tm
    nj = D_FF // tf
    kern = functools.partial(_ffn_kernel, tm=tm, sub=sub, nj=nj)
    mrow = lambda k: pl.BlockSpec((None, 1, D_MODEL), lambda i, j: ((i // tiles_per_batch) * 6 + k, 0, 0))
    return pl.pallas_call(
        kern,
        out_shape=jax.ShapeDtypeStruct((t, D_MODEL), F32),
        grid=(t // tm, nj),
        in_specs=[pl.BlockSpec((tm, D_MODEL), lambda i, j: (i, 0)),
                  pl.BlockSpec((1, D_MODEL), lambda i, j: (0, 0)),
                  mrow(4), mrow(3), mrow(5),
                  pl.BlockSpec((D_MODEL, tf), lambda i, j: (0, j)),
                  pl.BlockSpec((D_MODEL, tf), lambda i, j: (0, nj + j)),
                  pl.BlockSpec((tf, D_MODEL), lambda i, j: (j, 0))],
        out_specs=pl.BlockSpec((tm, D_MODEL), lambda i, j: (i, 0)),
        scratch_shapes=[pltpu.VMEM((tm, D_MODEL), BF16)],
        compiler_params=_cparams(("parallel", "arbitrary"), VMEM_LIMIT_BIG),
        name="ffn",
    )(x1, nw, mod3, mod3, mod3, wgu_bf, wgu_bf, wd_bf)


def _rope_partner_index():
    idx = np.arange(LANES)
    d = idx % DA_DH
    half = ROPE_DIM // 2
    return np.where(d < half, idx + half, np.where(d < ROPE_DIM, idx - half, idx))


def kernel(x, c, positions, norm1_w, norm2_w, w_ada, b_ada, w_in, mlstm_conv_w, mlstm_conv_b, mlstm_gate_b,
           mlstm_norm_w, q_norm_w, k_norm_w, lambda_q1, lambda_k1, lambda_q2, lambda_k2, subln_w, w_out,
           w_gate_up, w_down):
    B, S, D = x.shape
    T = B * S
    depth = w_in.shape[0]
    xf = x.reshape(T, D)
    c8 = jnp.pad(c, ((0, 8 - B), (0, 0)))

    inv_freq = ROPE_THETA ** (-jnp.arange(0, ROPE_DIM, 2, dtype=F32) / ROPE_DIM)
    invf = jnp.broadcast_to(inv_freq[:, None], (ROPE_DIM // 2, LANES))
    pidx = _rope_partner_index()

    for layer in range(depth):
        lambda_init = 0.8 - 0.6 * math.exp(-0.3 * layer)
        mod = _ada(c8, w_ada[layer], b_ada[layer].reshape(1, -1))
        mod3 = mod[:B].reshape(B * 6, 1, D)

        w_l = w_in[layer]
        w_t = w_l.T
        w_a = w_t[:GATE_LO].astype(BF16)
        w_b = w_t[GATE_HI:].astype(BF16)
        wgate = jnp.pad(w_t[GATE_LO:GATE_HI], ((0, LANES - 2 * ML_HEADS), (0, 0)))
        wg_hi = wgate.astype(BF16)
        wg_lo = (wgate - wg_hi.astype(F32)).astype(BF16)
        wg2 = jnp.stack([wg_hi, wg_lo])

        p, g = _inproj(xf, norm1_w[layer].reshape(1, D), mod3, w_a, w_b, wg2, S)

        gb = jnp.pad(mlstm_gate_b[layer].reshape(1, -1), ((0, 0), (0, LANES - 2 * ML_HEADS)))
        hml = _mlstm(p, g, mlstm_conv_w[layer], mlstm_conv_b[layer].reshape(1, -1), gb,
                     mlstm_norm_w[layer].reshape(1, ML_WIDTH), B, S)

        qw = jnp.tile(q_norm_w[layer], 2).reshape(1, LANES)
        kw = jnp.tile(k_norm_w[layer], 2).reshape(1, LANES)
        qn, kn = _qkprep(positions, p, invf, qw, qw[:, pidx], kw, kw[:, pidx])

        r64 = lambda a: a[layer].reshape(1, DA_DH).astype(F32)
        oda = _attn(qn, kn, p, r64(lambda_q1), r64(lambda_k1), r64(lambda_q2), r64(lambda_k2),
                    subln_w[layer].reshape(1, DA_DV), B, S, lambda_init)

        x1 = _outproj(xf, mod3, hml, oda, w_out[layer].astype(BF16), S)
        xf = _ffn(x1, norm2_w[layer].reshape(1, D), mod3, w_gate_up[layer].astype(BF16),
                  w_down[layer].astype(BF16), S)
    return xf.reshape(B, S, D)
```

```python
import functools
import math

import jax
import jax.numpy as jnp
import numpy as np
from jax import lax
from jax.experimental import pallas as pl
from jax.experimental.pallas import tpu as pltpu

F32 = jnp.float32
BF16 = jnp.bfloat16

D_MODEL = 2048
CHUNK = 64
ML_HEADS = 4
ML_DQK = 128
ML_DV = 256
ML_CONV = 4
GATE_CAP = 15.0
DA_HEADS = 8
DA_DH = 64
DA_DV = 128
ROPE_DIM = 16
ROPE_THETA = 500000.0
ML_WIDTH = ML_HEADS * ML_DV
DA_WIDTH = DA_HEADS * DA_DV
QK_W = ML_HEADS * ML_DQK
D_FF = 5632
EPS = 1e-6
NEG = -1e30
LOG2E = 1.4426950408889634

LANES = 128
VMEM_LIMIT = 52 * 1024 * 1024
VMEM_LIMIT_BIG = 60 * 1024 * 1024

P_COLS = 6144
GATE_LO = 3 * 1024
GATE_HI = GATE_LO + 2 * ML_HEADS

ML_L = 256
AT_T = 512


def _cparams(sem, vmem=VMEM_LIMIT):
    return pltpu.CompilerParams(dimension_semantics=sem, vmem_limit_bytes=vmem)


def _dot(a, b):
    return jnp.dot(a, b, preferred_element_type=F32)


def _dot_nt(a, b):
    return lax.dot_general(a, b, (((1,), (1,)), ((), ())), preferred_element_type=F32)


def _split3(x):
    hi = x.astype(BF16)
    r1 = x - hi.astype(F32)
    mid = r1.astype(BF16)
    lo = (r1 - mid.astype(F32)).astype(BF16)
    return hi, mid, lo


def _ada_kernel(c_ref, w_ref, b_ref, o_ref):
    c = c_ref[...]
    a = (c * jax.nn.sigmoid(c)).astype(BF16)
    o_ref[...] = _dot(a, w_ref[...].astype(BF16)) + b_ref[...]


def _ada(c8, w, b):
    n = w.shape[1]
    tn = 1024
    return pl.pallas_call(
        _ada_kernel,
        out_shape=jax.ShapeDtypeStruct((8, n), F32),
        grid=(n // tn,),
        in_specs=[pl.BlockSpec((8, D_MODEL), lambda j: (0, 0)),
                  pl.BlockSpec((D_MODEL, tn), lambda j: (0, j)),
                  pl.BlockSpec((1, tn), lambda j: (0, j))],
        out_specs=pl.BlockSpec((8, tn), lambda j: (0, j)),
        compiler_params=_cparams(("parallel",)),
        name="adaln",
    )(c8, w, b)


def _norm_mod(x, gain, sh):
    ms = jnp.mean(x * x, axis=-1, keepdims=True)
    return (x * lax.rsqrt(ms + EPS)) * gain + sh


def _norm_mod_rows(x_ref, nw_ref, sc_ref, sh_ref, emit, tm, sub=128):
    gain = nw_ref[...] * (1.0 + sc_ref[...])
    sh = sh_ref[...]

    def body(r, carry):
        off = pl.multiple_of(r * sub, sub)
        emit(off, _norm_mod(x_ref[pl.ds(off, sub), :], gain, sh))
        return carry

    lax.fori_loop(0, tm // sub, body, 0)


IN_TM = 1024
IN_TN = 1024
IN_SUB = 256
IN_NSUB = IN_TM // IN_SUB


def _inproj_kernel(x_ref, nw_ref, sc_ref, sh_ref, w_ref, wg_ref, o_ref, g_ref, ha_ref, hb_ref, *, ni):
    i = pl.program_id(0)
    j = pl.program_id(1)
    hbufs = (ha_ref, hb_ref)

    def norm_slice(dst):
        gain = nw_ref[...] * (1.0 + sc_ref[...])
        hh = _norm_mod(x_ref[...], gain, sh_ref[...]).astype(BF16)
        rows = pl.ds(pl.multiple_of(j * IN_SUB, IN_SUB), IN_SUB)
        dst[rows, :] = hh
        g_ref[rows, :] = _dot_nt(hh, wg_ref[...])

    def matmul(src):
        o_ref[...] = _dot_nt(src[...], w_ref[...]).astype(BF16)

    has_norm = (i < ni) & (j < IN_NSUB)
    has_mm = i > 0

    @pl.when(jnp.logical_not(has_mm) & has_norm)
    def _():
        norm_slice(ha_ref)

    for par in range(2):
        mine = (i % 2) == par

        @pl.when(mine & has_mm & has_norm)
        def _():
            norm_slice(hbufs[par])
            matmul(hbufs[1 - par])

        @pl.when(mine & has_mm & jnp.logical_not(has_norm))
        def _():
            matmul(hbufs[1 - par])


def _inproj(x2, nw, mod3, w_bf, wg, tpb):
    t = x2.shape[0]
    tm, tn = IN_TM, IN_TN
    tiles_per_batch = tpb // tm
    ni = t // tm
    last = ni - 1
    kern = functools.partial(_inproj_kernel, ni=ni)
    mrow = lambda k: pl.BlockSpec(
        (None, 1, D_MODEL), lambda i, j: ((jnp.minimum(i, last) // tiles_per_batch) * 6 + k, 0, 0))
    return pl.pallas_call(
        kern,
        out_shape=(jax.ShapeDtypeStruct((t, P_COLS), BF16), jax.ShapeDtypeStruct((t, LANES), F32)),
        grid=(ni + 1, P_COLS // tn),
        in_specs=[pl.BlockSpec((IN_SUB, D_MODEL),
                               lambda i, j: (jnp.minimum(i, last) * IN_NSUB + jnp.minimum(j, IN_NSUB - 1), 0)),
                  pl.BlockSpec((1, D_MODEL), lambda i, j: (0, 0)),
                  mrow(1), mrow(0),
                  pl.BlockSpec((tn, D_MODEL), lambda i, j: (j, 0)),
                  pl.BlockSpec((LANES, D_MODEL), lambda i, j: (0, 0))],
        out_specs=(pl.BlockSpec((tm, tn), lambda i, j: (jnp.maximum(i - 1, 0), jnp.where(i == 0, 0, j))),
                   pl.BlockSpec((tm, LANES), lambda i, j: (jnp.minimum(i, last), 0))),
        scratch_shapes=[pltpu.VMEM((tm, D_MODEL), BF16), pltpu.VMEM((tm, D_MODEL), BF16)],
        compiler_params=_cparams(("arbitrary", "arbitrary")),
        name="inproj",
    )(x2, nw, mod3, mod3, w_bf, wg)


def _conv_silu(x, tail8, w4, b, row8):
    acc = b + w4[3:4, :] * x
    for k in (1, 2, 3):
        xr = pltpu.roll(x, k, 0)
        tr = pltpu.roll(tail8, k, 0)
        first = jnp.where(row8 < k, tr, xr[0:8, :])
        xs = jnp.concatenate([first, xr[8:, :]], axis=0)
        acc = acc + w4[3 - k:4 - k, :] * xs
    return acc * jax.nn.sigmoid(acc)


def _mlstm_kernel(qk_ref, v_ref, og_ref, g_ref, cw_ref, cb_ref, gb_ref, nw_ref, o_ref, ct_ref, tail_ref):
    L = ML_L
    c = pl.program_id(1)

    @pl.when(c == 0)
    def _():
        ct_ref[...] = jnp.zeros_like(ct_ref)
        tail_ref[...] = jnp.zeros_like(tail_ref)

    g = g_ref[...] + gb_ref[...]
    gc = GATE_CAP * jnp.tanh(g * (1.0 / GATE_CAP))
    lf_all = -jnp.log1p(jnp.exp(-gc))
    ri = lax.broadcasted_iota(jnp.int32, (L, L), 0)
    ci = lax.broadcasted_iota(jnp.int32, (L, L), 1)
    tri = ci <= ri
    trib = jnp.where(tri, 1.0, 0.0).astype(BF16)
    f_hi, f_mid, f_lo = _split3(lf_all)
    b_all = _dot(trib, f_hi) + _dot(trib, f_mid) + _dot(trib, f_lo)
    u_all = gc - pltpu.roll(b_all, LANES - ML_HEADS, 1)
    u_t = u_all.T

    row8 = lax.broadcasted_iota(jnp.int32, (8, LANES), 0)
    ones_b = jnp.ones((L, LANES), BF16)
    tails = tail_ref[...]

    for h in range(ML_HEADS):
        qs = slice(h * ML_DQK, (h + 1) * ML_DQK)
        ks = slice(QK_W + h * ML_DQK, QK_W + (h + 1) * ML_DQK)
        vs = slice(h * ML_DV, (h + 1) * ML_DV)
        q = _conv_silu(qk_ref[:, qs].astype(F32), tails[:, qs], cw_ref[:, qs], cb_ref[:, qs], row8)
        k = _conv_silu(qk_ref[:, ks].astype(F32), tails[:, ks], cw_ref[:, ks], cb_ref[:, ks], row8)
        q = q * (ML_DQK ** -0.5)

        b_col = jnp.broadcast_to(b_all[:, ML_HEADS + h:ML_HEADS + h + 1], (L, LANES))
        u_col = jnp.broadcast_to(u_all[:, h:h + 1], (L, LANES))
        b_end = b_all[L - 1:L, ML_HEADS + h:ML_HEADS + h + 1]
        u_row = u_t[h:h + 1, :]

        logd = jnp.concatenate([b_col, b_col], axis=1) + u_row
        d = jnp.exp(jnp.where(tri, logd, NEG))
        s = lax.dot_general(q.astype(BF16), k.astype(BF16), (((1,), (1,)), ((), ())),
                            preferred_element_type=F32) * d

        vaug = jnp.concatenate([v_ref[:, vs], ones_b], axis=1)
        ct = ct_ref[h]
        tot = _dot((q * jnp.exp(b_col)).astype(BF16), ct.astype(BF16)) + _dot(s.astype(BF16), vaug)
        den = tot[:, ML_DV:]
        den2 = jnp.concatenate([den, den], axis=1)
        hc = tot[:, :ML_DV] / jnp.maximum(jnp.abs(den2), 1.0)

        kw = (k * jnp.exp(b_end + u_col)).astype(BF16)
        upd = lax.dot_general(kw, vaug, (((0,), (0,)), ((), ())), preferred_element_type=F32)
        ct_ref[h] = jnp.exp(b_end) * ct + upd

        ms = jnp.mean(hc * hc, axis=-1, keepdims=True)
        hn = hc * lax.rsqrt(ms + EPS) * nw_ref[:, vs]
        o_ref[:, vs] = (hn * jax.nn.sigmoid(og_ref[:, vs].astype(F32))).astype(BF16)

    tail_ref[...] = qk_ref[L - 16:L, :].astype(F32)[8:16, :]


def _mlstm(p, g, cw, cb, gb, nw, batch, seq):
    t = p.shape[0]
    L = ML_L
    nc = seq // L
    return pl.pallas_call(
        _mlstm_kernel,
        out_shape=jax.ShapeDtypeStruct((t, ML_WIDTH), BF16),
        grid=(batch, nc),
        in_specs=[pl.BlockSpec((L, 1024), lambda b, c: (b * nc + c, 0)),
                  pl.BlockSpec((L, 1024), lambda b, c: (b * nc + c, 1)),
                  pl.BlockSpec((L, 1024), lambda b, c: (b * nc + c, 2)),
                  pl.BlockSpec((L, LANES), lambda b, c: (b * nc + c, 0)),
                  pl.BlockSpec((ML_CONV, 1024), lambda b, c: (0, 0)),
                  pl.BlockSpec((1, 1024), lambda b, c: (0, 0)),
                  pl.BlockSpec((1, LANES), lambda b, c: (0, 0)),
                  pl.BlockSpec((1, ML_WIDTH), lambda b, c: (0, 0))],
        out_specs=pl.BlockSpec((L, ML_WIDTH), lambda b, c: (b * nc + c, 0)),
        scratch_shapes=[pltpu.VMEM((ML_HEADS, ML_DQK, ML_DV + LANES), F32),
                        pltpu.VMEM((8, 1024), F32)],
        compiler_params=_cparams(("parallel", "arbitrary")),
        name="mlstm",
    )(p, p, p, g, cw, cb, gb, nw)


def _qkprep_kernel(pos_ref, q_ref, k_ref, invf_ref, qw_ref, qpw_ref, kw_ref, kpw_ref, qo_ref, ko_ref):
    half = ROPE_DIM // 2
    tm = q_ref.shape[0]
    pos = pos_ref[0].astype(F32)
    ang = jnp.concatenate([invf_ref[...]] * (tm // LANES), axis=1) * pos
    fi = lax.broadcasted_iota(jnp.int32, (4 * half, LANES), 0)
    li = lax.broadcasted_iota(jnp.int32, (4 * half, LANES), 1)
    dl = li & (DA_DH - 1)
    hit = (dl < ROPE_DIM) & ((dl & (half - 1)) == (fi & (half - 1))) & (fi < 3 * half)
    e_cos = jnp.where(hit, 1.0, 0.0).astype(BF16)
    e_sin = jnp.where(hit, jnp.where(dl < half, -1.0, 1.0), 0.0).astype(BF16)

    def expand(t8, e):
        hi, mid, lo = _split3(t8)
        parts = jnp.concatenate([hi.astype(F32), mid.astype(F32), lo.astype(F32), jnp.zeros_like(t8)], axis=0)
        return lax.dot_general(parts.astype(BF16), e, (((0,), (0,)), ((), ())), preferred_element_type=F32)

    lane = lax.broadcasted_iota(jnp.int32, (1, LANES), 1)
    cs = expand(jnp.cos(ang), e_cos) + jnp.where((lane & (DA_DH - 1)) < ROPE_DIM, 0.0, 1.0)
    sn = expand(jnp.sin(ang), e_sin)
    ri = lax.broadcasted_iota(jnp.int32, (LANES, LANES), 0)
    ci = lax.broadcasted_iota(jnp.int32, (LANES, LANES), 1)
    dc = ci & (DA_DH - 1)
    partner = jnp.where(dc < half, ci + half, jnp.where(dc < ROPE_DIM, ci - half, -1))
    perm = jnp.where(ri == partner, 1.0, 0.0).astype(BF16)
    mean_blk = jnp.where((ri >> 6) == (ci >> 6), 1.0 / DA_DH, 0.0).astype(BF16)

    def prep(x_ref, w_ref, pw_ref, o_ref, scale):
        a = (w_ref[...] * scale) * cs
        bc = (pw_ref[...] * scale) * sn
        for hh in range(DA_HEADS):
            cols = slice(hh * LANES, (hh + 1) * LANES)
            xb = x_ref[:, cols]
            xf = xb.astype(F32)
            r = lax.rsqrt(_dot((xf * xf).astype(BF16), mean_blk) + EPS)
            px = _dot(xb, perm)
            o_ref[:, cols] = (r * (xf * a + px * bc)).astype(BF16)

    prep(q_ref, qw_ref, qpw_ref, qo_ref, DA_DH ** -0.5 * LOG2E)
    prep(k_ref, kw_ref, kpw_ref, ko_ref, 1.0)


def _qkprep(positions, p, invf, qw, qpw, kw, kpw):
    t = p.shape[0]
    tm = 512
    pos3 = positions.reshape(t // tm, 1, tm)
    vec = pl.BlockSpec((1, LANES), lambda i: (0, 0))
    return pl.pallas_call(
        _qkprep_kernel,
        out_shape=(jax.ShapeDtypeStruct((t, 1024), BF16), jax.ShapeDtypeStruct((t, 1024), BF16)),
        grid=(t // tm,),
        in_specs=[pl.BlockSpec((1, 1, tm), lambda i: (i, 0, 0)),
                  pl.BlockSpec((tm, 1024), lambda i: (i, 3)),
                  pl.BlockSpec((tm, 1024), lambda i: (i, 4)),
                  pl.BlockSpec((ROPE_DIM // 2, LANES), lambda i: (0, 0)),
                  vec, vec, vec, vec],
        out_specs=(pl.BlockSpec((tm, 1024), lambda i: (i, 0)),
                   pl.BlockSpec((tm, 1024), lambda i: (i, 0))),
        compiler_params=_cparams(("parallel",)),
        name="qkprep",
    )(pos3, p, p, invf, qw, qpw, kw, kpw)


def _attn_kernel(q_ref, k_ref, v_ref, lq1_ref, lk1_ref, lq2_ref, lk2_ref, sw_ref, o_ref, m_ref, acc_ref,
                 sa_ref, sb_ref, *, lambda_init):
    T = AT_T
    qi = pl.program_id(2)
    q = q_ref[...]
    lane = lax.broadcasted_iota(jnp.int32, (T, LANES), 1)
    zero = jnp.zeros_like(q)
    qm = (jnp.where(lane < DA_DH, q, zero), jnp.where(lane >= DA_DH, q, zero))
    m_ref[...] = jnp.full(m_ref.shape, NEG, F32)
    acc_ref[...] = jnp.zeros(acc_ref.shape, F32)
    ones_b = jnp.ones((T, LANES), BF16)

    sbufs = (sa_ref, sb_ref)

    def score(j, buf, masked):
        off = pl.multiple_of(j * T, T)
        kb = k_ref[pl.ds(off, T), :]
        for mi in range(2):
            s = lax.dot_general(qm[mi], kb, (((1,), (1,)), ((), ())), preferred_element_type=F32)
            if masked:
                r = lax.broadcasted_iota(jnp.int32, (T, T), 0)
                cc = lax.broadcasted_iota(jnp.int32, (T, T), 1)
                s = jnp.where((cc >> 6) <= (r >> 6), s, NEG)
            sbufs[buf][mi] = s

    def consume(j, buf):
        off = pl.multiple_of(j * T, T)
        vaug = jnp.concatenate([v_ref[pl.ds(off, T), :], ones_b], axis=1)
        for mi in range(2):
            s = sbufs[buf][mi]
            m_old = m_ref[mi]
            m_new = jnp.maximum(m_old, jnp.max(s, axis=1, keepdims=True))
            al = jnp.exp2(m_old - m_new)
            p = jnp.exp2(s - jnp.concatenate([m_new] * (T // LANES), axis=1)).astype(BF16)
            acc_ref[mi] = acc_ref[mi] * jnp.concatenate([al, al], axis=1) + _dot(p, vaug)
            m_ref[mi] = m_new

    score(qi, 0, True)

    def pair(i, carry):
        score(2 * i, 1, False)
        consume(jnp.where(i == 0, qi, 2 * i - 1), 0)
        score(2 * i + 1, 0, False)
        consume(2 * i, 1)
        return carry

    npair = qi // 2
    lax.fori_loop(0, npair, pair, 0)
    last0 = jnp.where(npair == 0, qi, 2 * npair - 1)

    @pl.when(qi % 2 == 1)
    def _():
        score(qi - 1, 1, False)
        consume(last0, 0)
        consume(qi - 1, 1)

    @pl.when(qi % 2 == 0)
    def _():
        consume(last0, 0)

    acc0 = acc_ref[0]
    acc1 = acc_ref[1]
    o0 = acc0[:, :DA_DV] / acc0[:, DA_DV:]
    o1 = acc1[:, :DA_DV] / acc1[:, DA_DV:]
    lam = (jnp.exp(jnp.sum(lq1_ref[...] * lk1_ref[...], axis=-1, keepdims=True))
           - jnp.exp(jnp.sum(lq2_ref[...] * lk2_ref[...], axis=-1, keepdims=True)) + lambda_init)
    o = o0 - lam * o1
    ms = jnp.mean(o * o, axis=-1, keepdims=True)
    o_ref[...] = (o * lax.rsqrt(ms + EPS) * (sw_ref[...] * (1.0 - lambda_init))).astype(BF16)


def _attn(qn, kn, p, lq1, lk1, lq2, lk2, sw, batch, seq, lambda_init):
    t = qn.shape[0]
    T = AT_T
    nq = seq // T
    vcol0 = 5 * 1024 // LANES
    vec64 = pl.BlockSpec((1, DA_DH), lambda b, h, i: (0, 0))
    kern = functools.partial(_attn_kernel, lambda_init=lambda_init)
    return pl.pallas_call(
        kern,
        out_shape=jax.ShapeDtypeStruct((t, DA_WIDTH), BF16),
        grid=(batch, DA_HEADS, nq),
        in_specs=[pl.BlockSpec((T, LANES), lambda b, h, i: (b * nq + i, h)),
                  pl.BlockSpec((seq, LANES), lambda b, h, i: (b, h)),
                  pl.BlockSpec((seq, LANES), lambda b, h, i: (b, vcol0 + h)),
                  vec64, vec64, vec64, vec64,
                  pl.BlockSpec((1, DA_DV), lambda b, h, i: (0, 0))],
        out_specs=pl.BlockSpec((T, LANES), lambda b, h, i: (b * nq + i, h)),
        scratch_shapes=[pltpu.VMEM((2, T, LANES), F32), pltpu.VMEM((2, T, 2 * LANES), F32),
                        pltpu.VMEM((2, T, T), F32), pltpu.VMEM((2, T, T), F32)],
        compiler_params=_cparams(("parallel", "parallel", "arbitrary")),
        name="diffattn",
    )(qn, kn, p, lq1, lk1, lq2, lk2, sw)


def _outproj_kernel(x_ref, g_ref, a_ref, b_ref, wa_ref, wb_ref, o_ref):
    mix = _dot(a_ref[...], wa_ref[...]) + _dot(b_ref[...], wb_ref[...])
    o_ref[...] = x_ref[...] + g_ref[...] * mix


def _outproj(x2, mod3, hml, oda, w_bf, tpb):
    t = x2.shape[0]
    tm = 512
    tiles_per_batch = tpb // tm
    return pl.pallas_call(
        _outproj_kernel,
        out_shape=jax.ShapeDtypeStruct((t, D_MODEL), F32),
        grid=(t // tm,),
        in_specs=[pl.BlockSpec((tm, D_MODEL), lambda i: (i, 0)),
                  pl.BlockSpec((None, 1, D_MODEL), lambda i: ((i // tiles_per_batch) * 6 + 2, 0, 0)),
                  pl.BlockSpec((tm, ML_WIDTH), lambda i: (i, 0)),
                  pl.BlockSpec((tm, DA_WIDTH), lambda i: (i, 0)),
                  pl.BlockSpec((ML_WIDTH, D_MODEL), lambda i: (0, 0)),
                  pl.BlockSpec((DA_WIDTH, D_MODEL), lambda i: (1, 0))],
        out_specs=pl.BlockSpec((tm, D_MODEL), lambda i: (i, 0)),
        compiler_params=_cparams(("parallel",)),
        name="outproj",
    )(x2, mod3, hml, oda, w_bf, w_bf)


def _ffn_kernel(x_ref, nw_ref, sc_ref, sh_ref, gt_ref, wg_ref, wu_ref, wd_ref, o_ref, h_ref, *, tm, sub, nj):
    j = pl.program_id(1)

    @pl.when(j == 0)
    def _():
        def emit(off, h):
            h_ref[pl.ds(off, sub), :] = h.astype(BF16)
        _norm_mod_rows(x_ref, nw_ref, sc_ref, sh_ref, emit, tm, sub)
        o_ref[...] = jnp.zeros_like(o_ref)

    h = h_ref[...]
    g = _dot(h, wg_ref[...])
    u = _dot(h, wu_ref[...])
    a = (g * jax.nn.sigmoid(g) * u).astype(BF16)
    o_ref[...] += _dot(a, wd_ref[...])

    @pl.when(j == nj - 1)
    def _():
        o_ref[...] = x_ref[...] + gt_ref[...] * o_ref[...]


def _ffn(x1, nw, mod3, wgu_bf, wd_bf, tpb):
    t = x1.shape[0]
    tm, tf, sub = 1024, 512, 128
    tiles_per_batch = tpb // tm
    nj = D_FF // tf
    kern = functools.partial(_ffn_kernel, tm=tm, sub=sub, nj=nj)
    mrow = lambda k: pl.BlockSpec((None, 1, D_MODEL), lambda i, j: ((i // tiles_per_batch) * 6 + k, 0, 0))
    return pl.pallas_call(
        kern,
        out_shape=jax.ShapeDtypeStruct((t, D_MODEL), F32),
        grid=(t // tm, nj),
        in_specs=[pl.BlockSpec((tm, D_MODEL), lambda i, j: (i, 0)),
                  pl.BlockSpec((1, D_MODEL), lambda i, j: (0, 0)),
                  mrow(4), mrow(3), mrow(5),
                  pl.BlockSpec((D_MODEL, tf), lambda i, j: (0, j)),
                  pl.BlockSpec((D_MODEL, tf), lambda i, j: (0, nj + j)),
                  pl.BlockSpec((tf, D_MODEL), lambda i, j: (j, 0))],
        out_specs=pl.BlockSpec((tm, D_MODEL), lambda i, j: (i, 0)),
        scratch_shapes=[pltpu.VMEM((tm, D_MODEL), BF16)],
        compiler_params=_cparams(("parallel", "arbitrary"), VMEM_LIMIT_BIG),
        name="ffn",
    )(x1, nw, mod3, mod3, mod3, wgu_bf, wgu_bf, wd_bf)


def _rope_partner_index():
    idx = np.arange(LANES)
    d = idx % DA_DH
    half = ROPE_DIM // 2
    return np.where(d < half, idx + half, np.where(d < ROPE_DIM, idx - half, idx))


def kernel(x, c, positions, norm1_w, norm2_w, w_ada, b_ada, w_in, mlstm_conv_w, mlstm_conv_b, mlstm_gate_b,
           mlstm_norm_w, q_norm_w, k_norm_w, lambda_q1, lambda_k1, lambda_q2, lambda_k2, subln_w, w_out,
           w_gate_up, w_down):
    B, S, D = x.shape
    T = B * S
    depth = w_in.shape[0]
    xf = x.reshape(T, D)
    c8 = jnp.pad(c, ((0, 8 - B), (0, 0)))

    inv_freq = ROPE_THETA ** (-jnp.arange(0, ROPE_DIM, 2, dtype=F32) / ROPE_DIM)
    invf = jnp.broadcast_to(inv_freq[:, None], (ROPE_DIM // 2, LANES))
    pidx = _rope_partner_index()

    for layer in range(depth):
        lambda_init = 0.8 - 0.6 * math.exp(-0.3 * layer)
        mod = _ada(c8, w_ada[layer], b_ada[layer].reshape(1, -1))
        mod3 = mod[:B].reshape(B * 6, 1, D)

        w_l = w_in[layer]
        w_t = w_l.T
        w_bf = jnp.concatenate([w_t[:GATE_LO], w_t[GATE_HI:]], axis=0).astype(BF16)
        wgate = jnp.pad(w_t[GATE_LO:GATE_HI], ((0, LANES - 2 * ML_HEADS), (0, 0)))
        wg = wgate.astype(BF16)

        p, g = _inproj(xf, norm1_w[layer].reshape(1, D), mod3, w_bf, wg, S)

        gb = jnp.pad(mlstm_gate_b[layer].reshape(1, -1), ((0, 0), (0, LANES - 2 * ML_HEADS)))
        hml = _mlstm(p, g, mlstm_conv_w[layer], mlstm_conv_b[layer].reshape(1, -1), gb,
                     mlstm_norm_w[layer].reshape(1, ML_WIDTH), B, S)

        qw = jnp.tile(q_norm_w[layer], 2).reshape(1, LANES)
        kw = jnp.tile(k_norm_w[layer], 2).reshape(1, LANES)
        qn, kn = _qkprep(positions, p, invf, qw, qw[:, pidx], kw, kw[:, pidx])

        r64 = lambda a: a[layer].reshape(1, DA_DH).astype(F32)
        oda = _attn(qn, kn, p, r64(lambda_q1), r64(lambda_k1), r64(lambda_q2), r64(lambda_k2),
                    subln_w[layer].reshape(1, DA_DV), B, S, lambda_init)

        x1 = _outproj(xf, mod3, hml, oda, w_out[layer].astype(BF16), S)
        xf = _ffn(x1, norm2_w[layer].reshape(1, D), mod3, w_gate_up[layer].astype(BF16),
                  w_down[layer].astype(BF16), S)
    return xf.reshape(B, S, D)
```

```python
import functools
import math

import jax
import jax.numpy as jnp
import numpy as np
from jax import lax
from jax.experimental import pallas as pl
from jax.experimental.pallas import tpu as pltpu

F32 = jnp.float32
BF16 = jnp.bfloat16

D_MODEL = 2048
CHUNK = 64
ML_HEADS = 4
ML_DQK = 128
ML_DV = 256
ML_CONV = 4
GATE_CAP = 15.0
DA_HEADS = 8
DA_DH = 64
DA_DV = 128
ROPE_DIM = 16
ROPE_THETA = 500000.0
ML_WIDTH = ML_HEADS * ML_DV
DA_WIDTH = DA_HEADS * DA_DV
QK_W = ML_HEADS * ML_DQK
D_FF = 5632
EPS = 1e-6
NEG = -1e30
LOG2E = 1.4426950408889634

LANES = 128
VMEM_LIMIT = 52 * 1024 * 1024
VMEM_LIMIT_BIG = 60 * 1024 * 1024

P_COLS = 6144
GATE_LO = 3 * 1024
GATE_HI = GATE_LO + 2 * ML_HEADS

ML_L = 256
AT_T = 512


def _cparams(sem, vmem=VMEM_LIMIT):
    return pltpu.CompilerParams(dimension_semantics=sem, vmem_limit_bytes=vmem)


def _dot(a, b):
    return jnp.dot(a, b, preferred_element_type=F32)


def _dot_nt(a, b):
    return lax.dot_general(a, b, (((1,), (1,)), ((), ())), preferred_element_type=F32)


def _split3(x):
    hi = x.astype(BF16)
    r1 = x - hi.astype(F32)
    mid = r1.astype(BF16)
    lo = (r1 - mid.astype(F32)).astype(BF16)
    return hi, mid, lo


def _ada_kernel(c_ref, w_ref, b_ref, o_ref):
    c = c_ref[...]
    a = (c * jax.nn.sigmoid(c)).astype(BF16)
    o_ref[...] = _dot(a, w_ref[...].astype(BF16)) + b_ref[...]


def _ada(c8, w, b):
    n = w.shape[1]
    tn = 1024
    return pl.pallas_call(
        _ada_kernel,
        out_shape=jax.ShapeDtypeStruct((8, n), F32),
        grid=(n // tn,),
        in_specs=[pl.BlockSpec((8, D_MODEL), lambda j: (0, 0)),
                  pl.BlockSpec((D_MODEL, tn), lambda j: (0, j)),
                  pl.BlockSpec((1, tn), lambda j: (0, j))],
        out_specs=pl.BlockSpec((8, tn), lambda j: (0, j)),
        compiler_params=_cparams(("parallel",)),
        name="adaln",
    )(c8, w, b)


def _norm_mod(x, gain, sh):
    ms = jnp.mean(x * x, axis=-1, keepdims=True)
    return (x * lax.rsqrt(ms + EPS)) * gain + sh


def _norm_mod_rows(x_ref, nw_ref, sc_ref, sh_ref, emit, tm, sub=128):
    gain = nw_ref[...] * (1.0 + sc_ref[...])
    sh = sh_ref[...]

    def body(r, carry):
        off = pl.multiple_of(r * sub, sub)
        emit(off, _norm_mod(x_ref[pl.ds(off, sub), :], gain, sh))
        return carry

    lax.fori_loop(0, tm // sub, body, 0)


IN_TM = 1024
IN_TN = 1024
IN_SUB = 256
IN_NSUB = IN_TM // IN_SUB


def _inproj_kernel(x_ref, nw_ref, sc_ref, sh_ref, w_ref, wg_ref, o_ref, g_ref, ha_ref, hb_ref, *, ni):
    i = pl.program_id(0)
    j = pl.program_id(1)
    hbufs = (ha_ref, hb_ref)

    def norm_slice(dst):
        gain = nw_ref[...] * (1.0 + sc_ref[...])
        hh = _norm_mod(x_ref[...], gain, sh_ref[...]).astype(BF16)
        rows = pl.ds(pl.multiple_of(j * IN_SUB, IN_SUB), IN_SUB)
        dst[rows, :] = hh
        g_ref[rows, :] = _dot_nt(hh, wg_ref[...])

    def matmul(src):
        o_ref[...] = _dot_nt(src[...], w_ref[...]).astype(BF16)

    has_norm = (i < ni) & (j < IN_NSUB)
    has_mm = i > 0

    @pl.when(jnp.logical_not(has_mm) & has_norm)
    def _():
        norm_slice(ha_ref)

    for par in range(2):
        mine = (i % 2) == par

        @pl.when(mine & has_mm & has_norm)
        def _():
            norm_slice(hbufs[par])
            matmul(hbufs[1 - par])

        @pl.when(mine & has_mm & jnp.logical_not(has_norm))
        def _():
            matmul(hbufs[1 - par])


def _inproj(x2, nw, mod3, w_bf, wg, tpb):
    t = x2.shape[0]
    tm, tn = IN_TM, IN_TN
    tiles_per_batch = tpb // tm
    ni = t // tm
    last = ni - 1
    kern = functools.partial(_inproj_kernel, ni=ni)
    mrow = lambda k: pl.BlockSpec(
        (None, 1, D_MODEL), lambda i, j: ((jnp.minimum(i, last) // tiles_per_batch) * 6 + k, 0, 0))
    return pl.pallas_call(
        kern,
        out_shape=(jax.ShapeDtypeStruct((t, P_COLS), BF16), jax.ShapeDtypeStruct((t, LANES), F32)),
        grid=(ni + 1, P_COLS // tn),
        in_specs=[pl.BlockSpec((IN_SUB, D_MODEL),
                               lambda i, j: (jnp.minimum(i, last) * IN_NSUB + jnp.minimum(j, IN_NSUB - 1), 0)),
                  pl.BlockSpec((1, D_MODEL), lambda i, j: (0, 0)),
                  mrow(1), mrow(0),
                  pl.BlockSpec((tn, D_MODEL), lambda i, j: (j, 0)),
                  pl.BlockSpec((LANES, D_MODEL), lambda i, j: (0, 0))],
        out_specs=(pl.BlockSpec((tm, tn), lambda i, j: (jnp.maximum(i - 1, 0), jnp.where(i == 0, 0, j))),
                   pl.BlockSpec((tm, LANES), lambda i, j: (jnp.minimum(i, last), 0))),
        scratch_shapes=[pltpu.VMEM((tm, D_MODEL), BF16), pltpu.VMEM((tm, D_MODEL), BF16)],
        compiler_params=_cparams(("arbitrary", "arbitrary")),
        name="inproj",
    )(x2, nw, mod3, mod3, w_bf, wg)


def _conv_silu(x, tail8, w4, b, row8):
    acc = b + w4[3:4, :] * x
    for k in (1, 2, 3):
        xr = pltpu.roll(x, k, 0)
        tr = pltpu.roll(tail8, k, 0)
        first = jnp.where(row8 < k, tr, xr[0:8, :])
        xs = jnp.concatenate([first, xr[8:, :]], axis=0)
        acc = acc + w4[3 - k:4 - k, :] * xs
    return acc * jax.nn.sigmoid(acc)


def _mlstm_kernel(qk_ref, v_ref, og_ref, g_ref, cw_ref, cb_ref, gb_ref, nw_ref, o_ref, ct_ref, tail_ref):
    L = ML_L
    c = pl.program_id(1)

    @pl.when(c == 0)
    def _():
        ct_ref[...] = jnp.zeros_like(ct_ref)
        tail_ref[...] = jnp.zeros_like(tail_ref)

    g = g_ref[...] + gb_ref[...]
    gc = GATE_CAP * jnp.tanh(g * (1.0 / GATE_CAP))
    lf_all = -jnp.log1p(jnp.exp(-gc))
    ri = lax.broadcasted_iota(jnp.int32, (L, L), 0)
    ci = lax.broadcasted_iota(jnp.int32, (L, L), 1)
    tri = ci <= ri
    trib = jnp.where(tri, 1.0, 0.0).astype(BF16)
    f_hi, f_mid, f_lo = _split3(lf_all)
    b_all = _dot(trib, f_hi) + _dot(trib, f_mid) + _dot(trib, f_lo)
    u_all = gc - pltpu.roll(b_all, LANES - ML_HEADS, 1)
    u_t = u_all.T

    row8 = lax.broadcasted_iota(jnp.int32, (8, LANES), 0)
    ones_b = jnp.ones((L, LANES), BF16)
    tails = tail_ref[...]

    for h in range(ML_HEADS):
        qs = slice(h * ML_DQK, (h + 1) * ML_DQK)
        ks = slice(QK_W + h * ML_DQK, QK_W + (h + 1) * ML_DQK)
        vs = slice(h * ML_DV, (h + 1) * ML_DV)
        q = _conv_silu(qk_ref[:, qs].astype(F32), tails[:, qs], cw_ref[:, qs], cb_ref[:, qs], row8)
        k = _conv_silu(qk_ref[:, ks].astype(F32), tails[:, ks], cw_ref[:, ks], cb_ref[:, ks], row8)
        q = q * (ML_DQK ** -0.5)

        b_col = jnp.broadcast_to(b_all[:, ML_HEADS + h:ML_HEADS + h + 1], (L, LANES))
        u_col = jnp.broadcast_to(u_all[:, h:h + 1], (L, LANES))
        b_end = b_all[L - 1:L, ML_HEADS + h:ML_HEADS + h + 1]
        u_row = u_t[h:h + 1, :]

        logd = jnp.concatenate([b_col, b_col], axis=1) + u_row
        d = jnp.exp(jnp.where(tri, logd, NEG))
        s = lax.dot_general(q.astype(BF16), k.astype(BF16), (((1,), (1,)), ((), ())),
                            preferred_element_type=F32) * d

        vaug = jnp.concatenate([v_ref[:, vs], ones_b], axis=1)
        ct = ct_ref[h]
        tot = _dot((q * jnp.exp(b_col)).astype(BF16), ct.astype(BF16)) + _dot(s.astype(BF16), vaug)
        den = tot[:, ML_DV:]
        den2 = jnp.concatenate([den, den], axis=1)
        hc = tot[:, :ML_DV] / jnp.maximum(jnp.abs(den2), 1.0)

        kw = (k * jnp.exp(b_end + u_col)).astype(BF16)
        upd = lax.dot_general(kw, vaug, (((0,), (0,)), ((), ())), preferred_element_type=F32)
        ct_ref[h] = jnp.exp(b_end) * ct + upd

        ms = jnp.mean(hc * hc, axis=-1, keepdims=True)
        hn = hc * lax.rsqrt(ms + EPS) * nw_ref[:, vs]
        o_ref[:, vs] = (hn * jax.nn.sigmoid(og_ref[:, vs].astype(F32))).astype(BF16)

    tail_ref[...] = qk_ref[L - 16:L, :].astype(F32)[8:16, :]


def _mlstm(p, g, cw, cb, gb, nw, batch, seq):
    t = p.shape[0]
    L = ML_L
    nc = seq // L
    return pl.pallas_call(
        _mlstm_kernel,
        out_shape=jax.ShapeDtypeStruct((t, ML_WIDTH), BF16),
        grid=(batch, nc),
        in_specs=[pl.BlockSpec((L, 1024), lambda b, c: (b * nc + c, 0)),
                  pl.BlockSpec((L, 1024), lambda b, c: (b * nc + c, 1)),
                  pl.BlockSpec((L, 1024), lambda b, c: (b * nc + c, 2)),
                  pl.BlockSpec((L, LANES), lambda b, c: (b * nc + c, 0)),
                  pl.BlockSpec((ML_CONV, 1024), lambda b, c: (0, 0)),
                  pl.BlockSpec((1, 1024), lambda b, c: (0, 0)),
                  pl.BlockSpec((1, LANES), lambda b, c: (0, 0)),
                  pl.BlockSpec((1, ML_WIDTH), lambda b, c: (0, 0))],
        out_specs=pl.BlockSpec((L, ML_WIDTH), lambda b, c: (b * nc + c, 0)),
        scratch_shapes=[pltpu.VMEM((ML_HEADS, ML_DQK, ML_DV + LANES), F32),
                        pltpu.VMEM((8, 1024), F32)],
        compiler_params=_cparams(("parallel", "arbitrary")),
        name="mlstm",
    )(p, p, p, g, cw, cb, gb, nw)


def _qkprep_kernel(pos_ref, q_ref, k_ref, invf_ref, qw_ref, qpw_ref, kw_ref, kpw_ref, qo_ref, ko_ref):
    half = ROPE_DIM // 2
    tm = q_ref.shape[0]
    pos = pos_ref[0].astype(F32)
    ang = jnp.concatenate([invf_ref[...]] * (tm // LANES), axis=1) * pos
    fi = lax.broadcasted_iota(jnp.int32, (4 * half, LANES), 0)
    li = lax.broadcasted_iota(jnp.int32, (4 * half, LANES), 1)
    dl = li & (DA_DH - 1)
    hit = (dl < ROPE_DIM) & ((dl & (half - 1)) == (fi & (half - 1))) & (fi < 3 * half)
    e_cos = jnp.where(hit, 1.0, 0.0).astype(BF16)
    e_sin = jnp.where(hit, jnp.where(dl < half, -1.0, 1.0), 0.0).astype(BF16)

    def expand(t8, e):
        hi, mid, lo = _split3(t8)
        parts = jnp.concatenate([hi.astype(F32), mid.astype(F32), lo.astype(F32), jnp.zeros_like(t8)], axis=0)
        return lax.dot_general(parts.astype(BF16), e, (((0,), (0,)), ((), ())), preferred_element_type=F32)

    lane = lax.broadcasted_iota(jnp.int32, (1, LANES), 1)
    cs = expand(jnp.cos(ang), e_cos) + jnp.where((lane & (DA_DH - 1)) < ROPE_DIM, 0.0, 1.0)
    sn = expand(jnp.sin(ang), e_sin)
    ri = lax.broadcasted_iota(jnp.int32, (LANES, LANES), 0)
    ci = lax.broadcasted_iota(jnp.int32, (LANES, LANES), 1)
    dc = ci & (DA_DH - 1)
    partner = jnp.where(dc < half, ci + half, jnp.where(dc < ROPE_DIM, ci - half, -1))
    perm = jnp.where(ri == partner, 1.0, 0.0).astype(BF16)
    mean_blk = jnp.where((ri >> 6) == (ci >> 6), 1.0 / DA_DH, 0.0).astype(BF16)

    def prep(x_ref, w_ref, pw_ref, o_ref, scale):
        a = (w_ref[...] * scale) * cs
        bc = (pw_ref[...] * scale) * sn
        for hh in range(DA_HEADS):
            cols = slice(hh * LANES, (hh + 1) * LANES)
            xb = x_ref[:, cols]
            xf = xb.astype(F32)
            r = lax.rsqrt(_dot((xf * xf).astype(BF16), mean_blk) + EPS)
            px = _dot(xb, perm)
            o_ref[:, cols] = (r * (xf * a + px * bc)).astype(BF16)

    prep(q_ref, qw_ref, qpw_ref, qo_ref, DA_DH ** -0.5 * LOG2E)
    prep(k_ref, kw_ref, kpw_ref, ko_ref, 1.0)


def _qkprep(positions, p, invf, qw, qpw, kw, kpw):
    t = p.shape[0]
    tm = 512
    pos3 = positions.reshape(t // tm, 1, tm)
    vec = pl.BlockSpec((1, LANES), lambda i: (0, 0))
    return pl.pallas_call(
        _qkprep_kernel,
        out_shape=(jax.ShapeDtypeStruct((t, 1024), BF16), jax.ShapeDtypeStruct((t, 1024), BF16)),
        grid=(t // tm,),
        in_specs=[pl.BlockSpec((1, 1, tm), lambda i: (i, 0, 0)),
                  pl.BlockSpec((tm, 1024), lambda i: (i, 3)),
                  pl.BlockSpec((tm, 1024), lambda i: (i, 4)),
                  pl.BlockSpec((ROPE_DIM // 2, LANES), lambda i: (0, 0)),
                  vec, vec, vec, vec],
        out_specs=(pl.BlockSpec((tm, 1024), lambda i: (i, 0)),
                   pl.BlockSpec((tm, 1024), lambda i: (i, 0))),
        compiler_params=_cparams(("parallel",)),
        name="qkprep",
    )(pos3, p, p, invf, qw, qpw, kw, kpw)


AT_GROUP = 4


def _attn_kernel(q_ref, k_ref, v_ref, lq1_ref, lk1_ref, lq2_ref, lk2_ref, sw_ref, o_ref, m_ref, acc_ref,
                 sa_ref, sb_ref, *, lambda_init, nq):
    T = AT_T
    lane = lax.broadcasted_iota(jnp.int32, (T, LANES), 1)
    ones_b = jnp.ones((T, LANES), BF16)
    sbufs = (sa_ref, sb_ref)
    stages = [(qi, kt) for qi in range(nq) for kt in [qi] + list(range(qi))]

    def score(t):
        qi, kt = stages[t]
        q = q_ref[qi * T:(qi + 1) * T, :]
        kb = k_ref[kt * T:(kt + 1) * T, :]
        zero = jnp.zeros_like(q)
        for mi, qm in enumerate((jnp.where(lane < DA_DH, q, zero), jnp.where(lane >= DA_DH, q, zero))):
            s = _dot_nt(qm, kb)
            if kt == qi:
                r = lax.broadcasted_iota(jnp.int32, (T, T), 0)
                cc = lax.broadcasted_iota(jnp.int32, (T, T), 1)
                s = jnp.where((cc >> 6) <= (r >> 6), s, NEG)
            sbufs[t % 2][mi] = s

    def consume(t):
        qi, kt = stages[t]
        vaug = jnp.concatenate([v_ref[kt * T:(kt + 1) * T, :], ones_b], axis=1)
        for mi in range(2):
            s = sbufs[t % 2][mi]
            rmax = jnp.max(s, axis=1, keepdims=True)
            if kt == qi:
                m_new = jnp.broadcast_to(rmax, (T, LANES))
                p = jnp.exp2(s - jnp.concatenate([m_new] * (T // LANES), axis=1)).astype(BF16)
                acc_ref[mi] = _dot(p, vaug)
            else:
                m_old = m_ref[mi]
                m_new = jnp.maximum(m_old, rmax)
                al = jnp.exp2(m_old - m_new)
                p = jnp.exp2(s - jnp.concatenate([m_new] * (T // LANES), axis=1)).astype(BF16)
                acc_ref[mi] = acc_ref[mi] * jnp.concatenate([al, al], axis=1) + _dot(p, vaug)
            m_ref[mi] = m_new
        if t + 1 == len(stages) or stages[t + 1][0] != qi:
            finalize(qi)

    def finalize(qi):
        acc0 = acc_ref[0]
        acc1 = acc_ref[1]
        o0 = acc0[:, :DA_DV] / acc0[:, DA_DV:]
        o1 = acc1[:, :DA_DV] / acc1[:, DA_DV:]
        lam = (jnp.exp(jnp.sum(lq1_ref[...] * lk1_ref[...], axis=-1, keepdims=True))
               - jnp.exp(jnp.sum(lq2_ref[...] * lk2_ref[...], axis=-1, keepdims=True)) + lambda_init)
        o = o0 - lam * o1
        ms = jnp.mean(o * o, axis=-1, keepdims=True)
        o_ref[qi * T:(qi + 1) * T, :] = (o * lax.rsqrt(ms + EPS) * (sw_ref[...] * (1.0 - lambda_init))).astype(BF16)

    always = (pl.program_id(0) >= 0, pl.program_id(1) >= 0)
    score(0)
    for g0 in range(0, len(stages), AT_GROUP):
        @pl.when(always[(g0 // AT_GROUP) % 2])
        def _():
            for t in range(g0, min(g0 + AT_GROUP, len(stages))):
                if t + 1 < len(stages):
                    score(t + 1)
                consume(t)


def _attn(qn, kn, p, lq1, lk1, lq2, lk2, sw, batch, seq, lambda_init):
    t = qn.shape[0]
    T = AT_T
    vcol0 = 5 * 1024 // LANES
    vec64 = pl.BlockSpec((1, DA_DH), lambda b, h: (0, 0))
    kern = functools.partial(_attn_kernel, lambda_init=lambda_init, nq=seq // T)
    return pl.pallas_call(
        kern,
        out_shape=jax.ShapeDtypeStruct((t, DA_WIDTH), BF16),
        grid=(batch, DA_HEADS),
        in_specs=[pl.BlockSpec((seq, LANES), lambda b, h: (b, h)),
                  pl.BlockSpec((seq, LANES), lambda b, h: (b, h)),
                  pl.BlockSpec((seq, LANES), lambda b, h: (b, vcol0 + h)),
                  vec64, vec64, vec64, vec64,
                  pl.BlockSpec((1, DA_DV), lambda b, h: (0, 0))],
        out_specs=pl.BlockSpec((seq, LANES), lambda b, h: (b, h)),
        scratch_shapes=[pltpu.VMEM((2, T, LANES), F32), pltpu.VMEM((2, T, 2 * LANES), F32),
                        pltpu.VMEM((2, T, T), F32), pltpu.VMEM((2, T, T), F32)],
        compiler_params=_cparams(("parallel", "parallel")),
        name="diffattn",
    )(qn, kn, p, lq1, lk1, lq2, lk2, sw)


def _outproj_kernel(x_ref, g_ref, a_ref, b_ref, wa_ref, wb_ref, o_ref):
    mix = _dot(a_ref[...], wa_ref[...]) + _dot(b_ref[...], wb_ref[...])
    o_ref[...] = x_ref[...] + g_ref[...] * mix


def _outproj(x2, mod3, hml, oda, w_bf, tpb):
    t = x2.shape[0]
    tm = 512
    tiles_per_batch = tpb // tm
    return pl.pallas_call(
        _outproj_kernel,
        out_shape=jax.ShapeDtypeStruct((t, D_MODEL), F32),
        grid=(t // tm,),
        in_specs=[pl.BlockSpec((tm, D_MODEL), lambda i: (i, 0)),
                  pl.BlockSpec((None, 1, D_MODEL), lambda i: ((i // tiles_per_batch) * 6 + 2, 0, 0)),
                  pl.BlockSpec((tm, ML_WIDTH), lambda i: (i, 0)),
                  pl.BlockSpec((tm, DA_WIDTH), lambda i: (i, 0)),
                  pl.BlockSpec((ML_WIDTH, D_MODEL), lambda i: (0, 0)),
                  pl.BlockSpec((DA_WIDTH, D_MODEL), lambda i: (1, 0))],
        out_specs=pl.BlockSpec((tm, D_MODEL), lambda i: (i, 0)),
        compiler_params=_cparams(("parallel",)),
        name="outproj",
    )(x2, mod3, hml, oda, w_bf, w_bf)


def _ffn_kernel(x_ref, nw_ref, sc_ref, sh_ref, gt_ref, wg_ref, wu_ref, wd_ref, o_ref, h_ref, *, tm, sub, nj):
    j = pl.program_id(1)

    @pl.when(j == 0)
    def _():
        def emit(off, h):
            h_ref[pl.ds(off, sub), :] = h.astype(BF16)
        _norm_mod_rows(x_ref, nw_ref, sc_ref, sh_ref, emit, tm, sub)
        o_ref[...] = jnp.zeros_like(o_ref)

    h = h_ref[...]
    g = _dot(h, wg_ref[...])
    u = _dot(h, wu_ref[...])
    a = (g * jax.nn.sigmoid(g) * u).astype(BF16)
    o_ref[...] += _dot(a, wd_ref[...])

    @pl.when(j == nj - 1)
    def _():
        o_ref[...] = x_ref[...] + gt_ref[...] * o_ref[...]


def _ffn(x1, nw, mod3, wgu_bf, wd_bf, tpb):
    t = x1.shape[0]
    tm, tf, sub = 1024, 512, 128
    tiles_per_batch = tpb // tm
    nj = D_FF // tf
    kern = functools.partial(_ffn_kernel, tm=tm, sub=sub, nj=nj)
    mrow = lambda k: pl.BlockSpec((None, 1, D_MODEL), lambda i, j: ((i // tiles_per_batch) * 6 + k, 0, 0))
    return pl.pallas_call(
        kern,
        out_shape=jax.ShapeDtypeStruct((t, D_MODEL), F32),
        grid=(t // tm, nj),
        in_specs=[pl.BlockSpec((tm, D_MODEL), lambda i, j: (i, 0)),
                  pl.BlockSpec((1, D_MODEL), lambda i, j: (0, 0)),
                  mrow(4), mrow(3), mrow(5),
                  pl.BlockSpec((D_MODEL, tf), lambda i, j: (0, j)),
                  pl.BlockSpec((D_MODEL, tf), lambda i, j: (0, nj + j)),
                  pl.BlockSpec((tf, D_MODEL), lambda i, j: (j, 0))],
        out_specs=pl.BlockSpec((tm, D_MODEL), lambda i, j: (i, 0)),
        scratch_shapes=[pltpu.VMEM((tm, D_MODEL), BF16)],
        compiler_params=_cparams(("parallel", "arbitrary"), VMEM_LIMIT_BIG),
        name="ffn",
    )(x1, nw, mod3, mod3, mod3, wgu_bf, wgu_bf, wd_bf)


def _rope_partner_index():
    idx = np.arange(LANES)
    d = idx % DA_DH
    half = ROPE_DIM // 2
    return np.where(d < half, idx + half, np.where(d < ROPE_DIM, idx - half, idx))


def kernel(x, c, positions, norm1_w, norm2_w, w_ada, b_ada, w_in, mlstm_conv_w, mlstm_conv_b, mlstm_gate_b,
           mlstm_norm_w, q_norm_w, k_norm_w, lambda_q1, lambda_k1, lambda_q2, lambda_k2, subln_w, w_out,
           w_gate_up, w_down):
    B, S, D = x.shape
    T = B * S
    depth = w_in.shape[0]
    xf = x.reshape(T, D)
    c8 = jnp.pad(c, ((0, 8 - B), (0, 0)))

    inv_freq = ROPE_THETA ** (-jnp.arange(0, ROPE_DIM, 2, dtype=F32) / ROPE_DIM)
    invf = jnp.broadcast_to(inv_freq[:, None], (ROPE_DIM // 2, LANES))
    pidx = _rope_partner_index()

    for layer in range(depth):
        lambda_init = 0.8 - 0.6 * math.exp(-0.3 * layer)
        mod = _ada(c8, w_ada[layer], b_ada[layer].reshape(1, -1))
        mod3 = mod[:B].reshape(B * 6, 1, D)

        w_l = w_in[layer]
        w_t = w_l.T
        w_bf = jnp.concatenate([w_t[:GATE_LO], w_t[GATE_HI:]], axis=0).astype(BF16)
        wgate = jnp.pad(w_t[GATE_LO:GATE_HI], ((0, LANES - 2 * ML_HEADS), (0, 0)))
        wg = wgate.astype(BF16)

        p, g = _inproj(xf, norm1_w[layer].reshape(1, D), mod3, w_bf, wg, S)

        gb = jnp.pad(mlstm_gate_b[layer].reshape(1, -1), ((0, 0), (0, LANES - 2 * ML_HEADS)))
        hml = _mlstm(p, g, mlstm_conv_w[layer], mlstm_conv_b[layer].reshape(1, -1), gb,
                     mlstm_norm_w[layer].reshape(1, ML_WIDTH), B, S)

        qw = jnp.tile(q_norm_w[layer], 2).reshape(1, LANES)
        kw = jnp.tile(k_norm_w[layer], 2).reshape(1, LANES)
        qn, kn = _qkprep(positions, p, invf, qw, qw[:, pidx], kw, kw[:, pidx])

        r64 = lambda a: a[layer].reshape(1, DA_DH).astype(F32)
        oda = _attn(qn, kn, p, r64(lambda_q1), r64(lambda_k1), r64(lambda_q2), r64(lambda_k2),
                    subln_w[layer].reshape(1, DA_DV), B, S, lambda_init)

        x1 = _outproj(xf, mod3, hml, oda, w_out[layer].astype(BF16), S)
        xf = _ffn(x1, norm2_w[layer].reshape(1, D), mod3, w_gate_up[layer].astype(BF16),
                  w_down[layer].astype(BF16), S)
    return xf.reshape(B, S, D)
```

```python
import functools
import math

import jax
import jax.numpy as jnp
import numpy as np
from jax import lax
from jax.experimental import pallas as pl
from jax.experimental.pallas import tpu as pltpu

F32 = jnp.float32
BF16 = jnp.bfloat16

D_MODEL = 2048
CHUNK = 64
ML_HEADS = 4
ML_DQK = 128
ML_DV = 256
ML_CONV = 4
GATE_CAP = 15.0
DA_HEADS = 8
DA_DH = 64
DA_DV = 128
ROPE_DIM = 16
ROPE_THETA = 500000.0
ML_WIDTH = ML_HEADS * ML_DV
DA_WIDTH = DA_HEADS * DA_DV
QK_W = ML_HEADS * ML_DQK
D_FF = 5632
EPS = 1e-6
NEG = -1e30
LOG2E = 1.4426950408889634

LANES = 128
VMEM_LIMIT = 52 * 1024 * 1024
VMEM_LIMIT_BIG = 60 * 1024 * 1024

P_COLS = 6144
GATE_LO = 3 * 1024
GATE_HI = GATE_LO + 2 * ML_HEADS

ML_L = 256
AT_T = 512


def _cparams(sem, vmem=VMEM_LIMIT):
    return pltpu.CompilerParams(dimension_semantics=sem, vmem_limit_bytes=vmem)


def _dot(a, b):
    return jnp.dot(a, b, preferred_element_type=F32)


def _dot_nt(a, b):
    return lax.dot_general(a, b, (((1,), (1,)), ((), ())), preferred_element_type=F32)


def _split3(x):
    hi = x.astype(BF16)
    r1 = x - hi.astype(F32)
    mid = r1.astype(BF16)
    lo = (r1 - mid.astype(F32)).astype(BF16)
    return hi, mid, lo


def _ada_kernel(c_ref, w_ref, b_ref, o_ref):
    c = c_ref[...]
    a = (c * jax.nn.sigmoid(c)).astype(BF16)
    o_ref[...] = _dot(a, w_ref[...].astype(BF16)) + b_ref[...]


def _ada(c8, w, b):
    n = w.shape[1]
    tn = 1024
    return pl.pallas_call(
        _ada_kernel,
        out_shape=jax.ShapeDtypeStruct((8, n), F32),
        grid=(n // tn,),
        in_specs=[pl.BlockSpec((8, D_MODEL), lambda j: (0, 0)),
                  pl.BlockSpec((D_MODEL, tn), lambda j: (0, j)),
                  pl.BlockSpec((1, tn), lambda j: (0, j))],
        out_specs=pl.BlockSpec((8, tn), lambda j: (0, j)),
        compiler_params=_cparams(("parallel",)),
        name="adaln",
    )(c8, w, b)


def _norm_mod(x, gain, sh):
    ms = jnp.mean(x * x, axis=-1, keepdims=True)
    return (x * lax.rsqrt(ms + EPS)) * gain + sh


def _norm_mod_rows(x_ref, nw_ref, sc_ref, sh_ref, emit, tm, sub=128):
    gain = nw_ref[...] * (1.0 + sc_ref[...])
    sh = sh_ref[...]

    def body(r, carry):
        off = pl.multiple_of(r * sub, sub)
        emit(off, _norm_mod(x_ref[pl.ds(off, sub), :], gain, sh))
        return carry

    lax.fori_loop(0, tm // sub, body, 0)


IN_TM = 1024
IN_TN = 1024
IN_SUB = 256
IN_NSUB = IN_TM // IN_SUB


def _inproj_kernel(x_ref, nw_ref, sc_ref, sh_ref, w_ref, wg_ref, o_ref, g_ref, ha_ref, hb_ref, *, ni):
    i = pl.program_id(0)
    j = pl.program_id(1)
    hbufs = (ha_ref, hb_ref)

    def norm_slice(dst):
        gain = nw_ref[...] * (1.0 + sc_ref[...])
        hh = _norm_mod(x_ref[...], gain, sh_ref[...]).astype(BF16)
        rows = pl.ds(pl.multiple_of(j * IN_SUB, IN_SUB), IN_SUB)
        dst[rows, :] = hh
        g_ref[rows, :] = _dot_nt(hh, wg_ref[...])

    def matmul(src):
        o_ref[...] = _dot_nt(src[...], w_ref[...]).astype(BF16)

    has_norm = (i < ni) & (j < IN_NSUB)
    has_mm = i > 0

    @pl.when(jnp.logical_not(has_mm) & has_norm)
    def _():
        norm_slice(ha_ref)

    for par in range(2):
        mine = (i % 2) == par

        @pl.when(mine & has_mm & has_norm)
        def _():
            norm_slice(hbufs[par])
            matmul(hbufs[1 - par])

        @pl.when(mine & has_mm & jnp.logical_not(has_norm))
        def _():
            matmul(hbufs[1 - par])


def _inproj(x2, nw, mod3, w_bf, wg, tpb):
    t = x2.shape[0]
    tm, tn = IN_TM, IN_TN
    tiles_per_batch = tpb // tm
    ni = t // tm
    last = ni - 1
    kern = functools.partial(_inproj_kernel, ni=ni)
    mrow = lambda k: pl.BlockSpec(
        (None, 1, D_MODEL), lambda i, j: ((jnp.minimum(i, last) // tiles_per_batch) * 6 + k, 0, 0))
    return pl.pallas_call(
        kern,
        out_shape=(jax.ShapeDtypeStruct((t, P_COLS), BF16), jax.ShapeDtypeStruct((t, LANES), F32)),
        grid=(ni + 1, P_COLS // tn),
        in_specs=[pl.BlockSpec((IN_SUB, D_MODEL),
                               lambda i, j: (jnp.minimum(i, last) * IN_NSUB + jnp.minimum(j, IN_NSUB - 1), 0)),
                  pl.BlockSpec((1, D_MODEL), lambda i, j: (0, 0)),
                  mrow(1), mrow(0),
                  pl.BlockSpec((tn, D_MODEL), lambda i, j: (j, 0)),
                  pl.BlockSpec((LANES, D_MODEL), lambda i, j: (0, 0))],
        out_specs=(pl.BlockSpec((tm, tn), lambda i, j: (jnp.maximum(i - 1, 0), jnp.where(i == 0, 0, j))),
                   pl.BlockSpec((tm, LANES), lambda i, j: (jnp.minimum(i, last), 0))),
        scratch_shapes=[pltpu.VMEM((tm, D_MODEL), BF16), pltpu.VMEM((tm, D_MODEL), BF16)],
        compiler_params=_cparams(("arbitrary", "arbitrary")),
        name="inproj",
    )(x2, nw, mod3, mod3, w_bf, wg)


def _wprep_kernel(w_hbm, o_ref, buf_ref, sem, *, tn, gate_lo, gate_rows):
    j = pl.program_id(0)
    src = pl.multiple_of(j * tn + jnp.where(j * tn >= gate_lo, gate_rows, 0), 8)
    copy = pltpu.make_async_copy(w_hbm.at[pl.ds(src, tn), :], buf_ref, sem)
    copy.start()
    copy.wait()
    o_ref[...] = buf_ref[...].astype(BF16)


def _wprep(w_t):
    tn = 512
    kern = functools.partial(_wprep_kernel, tn=tn, gate_lo=GATE_LO, gate_rows=GATE_HI - GATE_LO)
    return pl.pallas_call(
        kern,
        out_shape=jax.ShapeDtypeStruct((P_COLS, D_MODEL), BF16),
        grid=(P_COLS // tn,),
        in_specs=[pl.BlockSpec(memory_space=pl.ANY)],
        out_specs=pl.BlockSpec((tn, D_MODEL), lambda j: (j, 0)),
        scratch_shapes=[pltpu.VMEM((tn, D_MODEL), F32), pltpu.SemaphoreType.DMA(())],
        compiler_params=_cparams(("arbitrary",)),
        name="wprep",
    )(w_t)


def _conv_silu(x, tail8, w4, b, row8):
    acc = b + w4[3:4, :] * x
    for k in (1, 2, 3):
        xr = pltpu.roll(x, k, 0)
        tr = pltpu.roll(tail8, k, 0)
        first = jnp.where(row8 < k, tr, xr[0:8, :])
        xs = jnp.concatenate([first, xr[8:, :]], axis=0)
        acc = acc + w4[3 - k:4 - k, :] * xs
    return acc * jax.nn.sigmoid(acc)


def _mlstm_kernel(qk_ref, v_ref, og_ref, g_ref, cw_ref, cb_ref, gb_ref, nw_ref, o_ref, ct_ref, tail_ref):
    L = ML_L
    c = pl.program_id(1)

    @pl.when(c == 0)
    def _():
        ct_ref[...] = jnp.zeros_like(ct_ref)
        tail_ref[...] = jnp.zeros_like(tail_ref)

    g = g_ref[...] + gb_ref[...]
    gc = GATE_CAP * jnp.tanh(g * (1.0 / GATE_CAP))
    lf_all = -jnp.log1p(jnp.exp(-gc))
    ri = lax.broadcasted_iota(jnp.int32, (L, L), 0)
    ci = lax.broadcasted_iota(jnp.int32, (L, L), 1)
    tri = ci <= ri
    trib = jnp.where(tri, 1.0, 0.0).astype(BF16)
    f_hi, f_mid, f_lo = _split3(lf_all)
    b_all = _dot(trib, f_hi) + _dot(trib, f_mid) + _dot(trib, f_lo)
    u_all = gc - pltpu.roll(b_all, LANES - ML_HEADS, 1)
    u_t = u_all.T

    row8 = lax.broadcasted_iota(jnp.int32, (8, LANES), 0)
    ones_b = jnp.ones((L, LANES), BF16)
    tails = tail_ref[...]

    for h in range(ML_HEADS):
        qs = slice(h * ML_DQK, (h + 1) * ML_DQK)
        ks = slice(QK_W + h * ML_DQK, QK_W + (h + 1) * ML_DQK)
        vs = slice(h * ML_DV, (h + 1) * ML_DV)
        q = _conv_silu(qk_ref[:, qs].astype(F32), tails[:, qs], cw_ref[:, qs], cb_ref[:, qs], row8)
        k = _conv_silu(qk_ref[:, ks].astype(F32), tails[:, ks], cw_ref[:, ks], cb_ref[:, ks], row8)
        q = q * (ML_DQK ** -0.5)

        b_col = jnp.broadcast_to(b_all[:, ML_HEADS + h:ML_HEADS + h + 1], (L, LANES))
        u_col = jnp.broadcast_to(u_all[:, h:h + 1], (L, LANES))
        b_end = b_all[L - 1:L, ML_HEADS + h:ML_HEADS + h + 1]
        u_row = u_t[h:h + 1, :]

        logd = jnp.concatenate([b_col, b_col], axis=1) + u_row
        d = jnp.exp(jnp.where(tri, logd, NEG))
        s = lax.dot_general(q.astype(BF16), k.astype(BF16), (((1,), (1,)), ((), ())),
                            preferred_element_type=F32) * d

        vaug = jnp.concatenate([v_ref[:, vs], ones_b], axis=1)
        ct = ct_ref[h]
        tot = _dot((q * jnp.exp(b_col)).astype(BF16), ct.astype(BF16)) + _dot(s.astype(BF16), vaug)
        den = tot[:, ML_DV:]
        den2 = jnp.concatenate([den, den], axis=1)
        hc = tot[:, :ML_DV] / jnp.maximum(jnp.abs(den2), 1.0)

        kw = (k * jnp.exp(b_end + u_col)).astype(BF16)
        upd = lax.dot_general(kw, vaug, (((0,), (0,)), ((), ())), preferred_element_type=F32)
        ct_ref[h] = jnp.exp(b_end) * ct + upd

        ms = jnp.mean(hc * hc, axis=-1, keepdims=True)
        hn = hc * lax.rsqrt(ms + EPS) * nw_ref[:, vs]
        o_ref[:, vs] = (hn * jax.nn.sigmoid(og_ref[:, vs].astype(F32))).astype(BF16)

    tail_ref[...] = qk_ref[L - 16:L, :].astype(F32)[8:16, :]


def _mlstm(p, g, cw, cb, gb, nw, batch, seq):
    t = p.shape[0]
    L = ML_L
    nc = seq // L
    return pl.pallas_call(
        _mlstm_kernel,
        out_shape=jax.ShapeDtypeStruct((t, ML_WIDTH), BF16),
        grid=(batch, nc),
        in_specs=[pl.BlockSpec((L, 1024), lambda b, c: (b * nc + c, 0)),
                  pl.BlockSpec((L, 1024), lambda b, c: (b * nc + c, 1)),
                  pl.BlockSpec((L, 1024), lambda b, c: (b * nc + c, 2)),
                  pl.BlockSpec((L, LANES), lambda b, c: (b * nc + c, 0)),
                  pl.BlockSpec((ML_CONV, 1024), lambda b, c: (0, 0)),
                  pl.BlockSpec((1, 1024), lambda b, c: (0, 0)),
                  pl.BlockSpec((1, LANES), lambda b, c: (0, 0)),
                  pl.BlockSpec((1, ML_WIDTH), lambda b, c: (0, 0))],
        out_specs=pl.BlockSpec((L, ML_WIDTH), lambda b, c: (b * nc + c, 0)),
        scratch_shapes=[pltpu.VMEM((ML_HEADS, ML_DQK, ML_DV + LANES), F32),
                        pltpu.VMEM((8, 1024), F32)],
        compiler_params=_cparams(("parallel", "arbitrary")),
        name="mlstm",
    )(p, p, p, g, cw, cb, gb, nw)


def _qkprep_kernel(pos_ref, q_ref, k_ref, invf_ref, qw_ref, qpw_ref, kw_ref, kpw_ref, qo_ref, ko_ref):
    half = ROPE_DIM // 2
    tm = q_ref.shape[0]
    pos = pos_ref[0].astype(F32)
    ang = jnp.concatenate([invf_ref[...]] * (tm // LANES), axis=1) * pos
    fi = lax.broadcasted_iota(jnp.int32, (4 * half, LANES), 0)
    li = lax.broadcasted_iota(jnp.int32, (4 * half, LANES), 1)
    dl = li & (DA_DH - 1)
    hit = (dl < ROPE_DIM) & ((dl & (half - 1)) == (fi & (half - 1))) & (fi < 3 * half)
    e_cos = jnp.where(hit, 1.0, 0.0).astype(BF16)
    e_sin = jnp.where(hit, jnp.where(dl < half, -1.0, 1.0), 0.0).astype(BF16)

    def expand(t8, e):
        hi, mid, lo = _split3(t8)
        parts = jnp.concatenate([hi.astype(F32), mid.astype(F32), lo.astype(F32), jnp.zeros_like(t8)], axis=0)
        return lax.dot_general(parts.astype(BF16), e, (((0,), (0,)), ((), ())), preferred_element_type=F32)

    lane = lax.broadcasted_iota(jnp.int32, (1, LANES), 1)
    cs = expand(jnp.cos(ang), e_cos) + jnp.where((lane & (DA_DH - 1)) < ROPE_DIM, 0.0, 1.0)
    sn = expand(jnp.sin(ang), e_sin)
    ri = lax.broadcasted_iota(jnp.int32, (LANES, LANES), 0)
    ci = lax.broadcasted_iota(jnp.int32, (LANES, LANES), 1)
    dc = ci & (DA_DH - 1)
    partner = jnp.where(dc < half, ci + half, jnp.where(dc < ROPE_DIM, ci - half, -1))
    perm = jnp.where(ri == partner, 1.0, 0.0).astype(BF16)
    mean_blk = jnp.where((ri >> 6) == (ci >> 6), 1.0 / DA_DH, 0.0).astype(BF16)

    def prep(x_ref, w_ref, pw_ref, o_ref, scale):
        a = (w_ref[...] * scale) * cs
        bc = (pw_ref[...] * scale) * sn
        for hh in range(DA_HEADS):
            cols = slice(hh * LANES, (hh + 1) * LANES)
            xb = x_ref[:, cols]
            xf = xb.astype(F32)
            r = lax.rsqrt(_dot((xf * xf).astype(BF16), mean_blk) + EPS)
            px = _dot(xb, perm)
            o_ref[:, cols] = (r * (xf * a + px * bc)).astype(BF16)

    prep(q_ref, qw_ref, qpw_ref, qo_ref, DA_DH ** -0.5 * LOG2E)
    prep(k_ref, kw_ref, kpw_ref, ko_ref, 1.0)


def _qkprep(positions, p, invf, qw, qpw, kw, kpw):
    t = p.shape[0]
    tm = 512
    pos3 = positions.reshape(t // tm, 1, tm)
    vec = pl.BlockSpec((1, LANES), lambda i: (0, 0))
    return pl.pallas_call(
        _qkprep_kernel,
        out_shape=(jax.ShapeDtypeStruct((t, 1024), BF16), jax.ShapeDtypeStruct((t, 1024), BF16)),
        grid=(t // tm,),
        in_specs=[pl.BlockSpec((1, 1, tm), lambda i: (i, 0, 0)),
                  pl.BlockSpec((tm, 1024), lambda i: (i, 3)),
                  pl.BlockSpec((tm, 1024), lambda i: (i, 4)),
                  pl.BlockSpec((ROPE_DIM // 2, LANES), lambda i: (0, 0)),
                  vec, vec, vec, vec],
        out_specs=(pl.BlockSpec((tm, 1024), lambda i: (i, 0)),
                   pl.BlockSpec((tm, 1024), lambda i: (i, 0))),
        compiler_params=_cparams(("parallel",)),
        name="qkprep",
    )(pos3, p, p, invf, qw, qpw, kw, kpw)


AT_GROUP = 4


def _attn_kernel(q_ref, k_ref, v_ref, lq1_ref, lk1_ref, lq2_ref, lk2_ref, sw_ref, o_ref, m_ref, acc_ref,
                 sa_ref, sb_ref, *, lambda_init, nq):
    T = AT_T
    lane = lax.broadcasted_iota(jnp.int32, (T, LANES), 1)
    ones_b = jnp.ones((T, LANES), BF16)
    sbufs = (sa_ref, sb_ref)
    stages = [(qi, kt) for qi in range(nq) for kt in [qi] + list(range(qi))]

    H = T // 2
    nlt = T // LANES

    def chunk_mask(rows, cols, row0):
        r = lax.broadcasted_iota(jnp.int32, (rows, cols), 0) + row0
        cc = lax.broadcasted_iota(jnp.int32, (rows, cols), 1)
        return (cc >> 6) <= (r >> 6)

    def score(t):
        qi, kt = stages[t]
        q = q_ref[qi * T:(qi + 1) * T, :]
        kb = k_ref[kt * T:(kt + 1) * T, :]
        zero = jnp.zeros_like(q)
        for mi, qm in enumerate((jnp.where(lane < DA_DH, q, zero), jnp.where(lane >= DA_DH, q, zero))):
            if kt == qi:
                sbufs[t % 2][mi, :H, :H] = jnp.where(chunk_mask(H, H, 0), _dot_nt(qm[:H], kb[:H]), NEG)
                sbufs[t % 2][mi, H:, :] = jnp.where(chunk_mask(H, T, H), _dot_nt(qm[H:], kb), NEG)
            else:
                sbufs[t % 2][mi] = _dot_nt(qm, kb)

    def softmax_pv(s, m_old, acc_old, vaug):
        rmax = jnp.max(s, axis=1, keepdims=True)
        ntile = s.shape[1] // LANES
        if m_old is None:
            m_new = jnp.broadcast_to(rmax, (s.shape[0], LANES))
            p = jnp.exp2(s - jnp.concatenate([m_new] * ntile, axis=1)).astype(BF16)
            return m_new, _dot(p, vaug)
        m_new = jnp.maximum(m_old, rmax)
        al = jnp.exp2(m_old - m_new)
        p = jnp.exp2(s - jnp.concatenate([m_new] * ntile, axis=1)).astype(BF16)
        return m_new, acc_old * jnp.concatenate([al, al], axis=1) + _dot(p, vaug)

    def consume(t):
        qi, kt = stages[t]
        vaug = jnp.concatenate([v_ref[kt * T:(kt + 1) * T, :], ones_b], axis=1)
        for mi in range(2):
            if kt == qi:
                m_ref[mi, :H], acc_ref[mi, :H] = softmax_pv(sbufs[t % 2][mi, :H, :H], None, None, vaug[:H])
                m_ref[mi, H:], acc_ref[mi, H:] = softmax_pv(sbufs[t % 2][mi, H:, :], None, None, vaug)
            else:
                m_ref[mi], acc_ref[mi] = softmax_pv(sbufs[t % 2][mi], m_ref[mi], acc_ref[mi], vaug)
        if t + 1 == len(stages) or stages[t + 1][0] != qi:
            finalize(qi)

    def finalize(qi):
        acc0 = acc_ref[0]
        acc1 = acc_ref[1]
        o0 = acc0[:, :DA_DV] / acc0[:, DA_DV:]
        o1 = acc1[:, :DA_DV] / acc1[:, DA_DV:]
        lam = (jnp.exp(jnp.sum(lq1_ref[...] * lk1_ref[...], axis=-1, keepdims=True))
               - jnp.exp(jnp.sum(lq2_ref[...] * lk2_ref[...], axis=-1, keepdims=True)) + lambda_init)
        o = o0 - lam * o1
        ms = jnp.mean(o * o, axis=-1, keepdims=True)
        o_ref[qi * T:(qi + 1) * T, :] = (o * lax.rsqrt(ms + EPS) * (sw_ref[...] * (1.0 - lambda_init))).astype(BF16)

    always = (pl.program_id(0) >= 0, pl.program_id(1) >= 0)
    score(0)
    for g0 in range(0, len(stages), AT_GROUP):
        @pl.when(always[(g0 // AT_GROUP) % 2])
        def _():
            for t in range(g0, min(g0 + AT_GROUP, len(stages))):
                if t + 1 < len(stages):
                    score(t + 1)
                consume(t)


def _attn(qn, kn, p, lq1, lk1, lq2, lk2, sw, batch, seq, lambda_init):
    t = qn.shape[0]
    T = AT_T
    vcol0 = 5 * 1024 // LANES
    vec64 = pl.BlockSpec((1, DA_DH), lambda b, h: (0, 0))
    kern = functools.partial(_attn_kernel, lambda_init=lambda_init, nq=seq // T)
    return pl.pallas_call(
        kern,
        out_shape=jax.ShapeDtypeStruct((t, DA_WIDTH), BF16),
        grid=(batch, DA_HEADS),
        in_specs=[pl.BlockSpec((seq, LANES), lambda b, h: (b, h)),
                  pl.BlockSpec((seq, LANES), lambda b, h: (b, h)),
                  pl.BlockSpec((seq, LANES), lambda b, h: (b, vcol0 + h)),
                  vec64, vec64, vec64, vec64,
                  pl.BlockSpec((1, DA_DV), lambda b, h: (0, 0))],
        out_specs=pl.BlockSpec((seq, LANES), lambda b, h: (b, h)),
        scratch_shapes=[pltpu.VMEM((2, T, LANES), F32), pltpu.VMEM((2, T, 2 * LANES), F32),
                        pltpu.VMEM((2, T, T), F32), pltpu.VMEM((2, T, T), F32)],
        compiler_params=_cparams(("parallel", "parallel")),
        name="diffattn",
    )(qn, kn, p, lq1, lk1, lq2, lk2, sw)


def _outproj_kernel(x_ref, g_ref, a_ref, b_ref, wa_ref, wb_ref, o_ref):
    mix = _dot(a_ref[...], wa_ref[...]) + _dot(b_ref[...], wb_ref[...])
    o_ref[...] = x_ref[...] + g_ref[...] * mix


def _outproj(x2, mod3, hml, oda, w_bf, tpb):
    t = x2.shape[0]
    tm = 512
    tiles_per_batch = tpb // tm
    return pl.pallas_call(
        _outproj_kernel,
        out_shape=jax.ShapeDtypeStruct((t, D_MODEL), F32),
        grid=(t // tm,),
        in_specs=[pl.BlockSpec((tm, D_MODEL), lambda i: (i, 0)),
                  pl.BlockSpec((None, 1, D_MODEL), lambda i: ((i // tiles_per_batch) * 6 + 2, 0, 0)),
                  pl.BlockSpec((tm, ML_WIDTH), lambda i: (i, 0)),
                  pl.BlockSpec((tm, DA_WIDTH), lambda i: (i, 0)),
                  pl.BlockSpec((ML_WIDTH, D_MODEL), lambda i: (0, 0)),
                  pl.BlockSpec((DA_WIDTH, D_MODEL), lambda i: (1, 0))],
        out_specs=pl.BlockSpec((tm, D_MODEL), lambda i: (i, 0)),
        compiler_params=_cparams(("parallel",)),
        name="outproj",
    )(x2, mod3, hml, oda, w_bf, w_bf)


def _ffn_kernel(x_ref, nw_ref, sc_ref, sh_ref, gt_ref, wg_ref, wu_ref, wd_ref, o_ref, h_ref, *, tm, sub, nj):
    j = pl.program_id(1)

    @pl.when(j == 0)
    def _():
        def emit(off, h):
            h_ref[pl.ds(off, sub), :] = h.astype(BF16)
        _norm_mod_rows(x_ref, nw_ref, sc_ref, sh_ref, emit, tm, sub)

    def down():
        h = h_ref[...]
        g = _dot(h, wg_ref[...])
        u = _dot(h, wu_ref[...])
        a = (g * jax.nn.sigmoid(g) * u).astype(BF16)
        return _dot(a, wd_ref[...])

    @pl.when(j == 0)
    def _():
        o_ref[...] = down()

    @pl.when(j > 0)
    def _():
        o_ref[...] += down()

    @pl.when(j == nj - 1)
    def _():
        o_ref[...] = x_ref[...] + gt_ref[...] * o_ref[...]


def _ffn(x1, nw, mod3, wgu_bf, wd_bf, tpb):
    t = x1.shape[0]
    tm, tf, sub = 1024, 512, 128
    tiles_per_batch = tpb // tm
    nj = D_FF // tf
    kern = functools.partial(_ffn_kernel, tm=tm, sub=sub, nj=nj)
    mrow = lambda k: pl.BlockSpec((None, 1, D_MODEL), lambda i, j: ((i // tiles_per_batch) * 6 + k, 0, 0))
    return pl.pallas_call(
        kern,
        out_shape=jax.ShapeDtypeStruct((t, D_MODEL), F32),
        grid=(t // tm, nj),
        in_specs=[pl.BlockSpec((tm, D_MODEL), lambda i, j: (i, 0)),
                  pl.BlockSpec((1, D_MODEL), lambda i, j: (0, 0)),
                  mrow(4), mrow(3), mrow(5),
                  pl.BlockSpec((D_MODEL, tf), lambda i, j: (0, j)),
                  pl.BlockSpec((D_MODEL, tf), lambda i, j: (0, nj + j)),
                  pl.BlockSpec((tf, D_MODEL), lambda i, j: (j, 0))],
        out_specs=pl.BlockSpec((tm, D_MODEL), lambda i, j: (i, 0)),
        scratch_shapes=[pltpu.VMEM((tm, D_MODEL), BF16)],
        compiler_params=_cparams(("parallel", "arbitrary"), VMEM_LIMIT_BIG),
        name="ffn",
    )(x1, nw, mod3, mod3, mod3, wgu_bf, wgu_bf, wd_bf)


def _rope_partner_index():
    idx = np.arange(LANES)
    d = idx % DA_DH
    half = ROPE_DIM // 2
    return np.where(d < half, idx + half, np.where(d < ROPE_DIM, idx - half, idx))


def kernel(x, c, positions, norm1_w, norm2_w, w_ada, b_ada, w_in, mlstm_conv_w, mlstm_conv_b, mlstm_gate_b,
           mlstm_norm_w, q_norm_w, k_norm_w, lambda_q1, lambda_k1, lambda_q2, lambda_k2, subln_w, w_out,
           w_gate_up, w_down):
    B, S, D = x.shape
    T = B * S
    depth = w_in.shape[0]
    xf = x.reshape(T, D)
    c8 = jnp.pad(c, ((0, 8 - B), (0, 0)))

    inv_freq = ROPE_THETA ** (-jnp.arange(0, ROPE_DIM, 2, dtype=F32) / ROPE_DIM)
    invf = jnp.broadcast_to(inv_freq[:, None], (ROPE_DIM // 2, LANES))
    pidx = _rope_partner_index()

    for layer in range(depth):
        lambda_init = 0.8 - 0.6 * math.exp(-0.3 * layer)
        mod = _ada(c8, w_ada[layer], b_ada[layer].reshape(1, -1))
        mod3 = mod[:B].reshape(B * 6, 1, D)

        w_l = w_in[layer]
        w_t = w_l.T
        w_bf = _wprep(w_t)
        wgate = jnp.pad(w_t[GATE_LO:GATE_HI], ((0, LANES - 2 * ML_HEADS), (0, 0)))
        wg = wgate.astype(BF16)

        p, g = _inproj(xf, norm1_w[layer].reshape(1, D), mod3, w_bf, wg, S)

        gb = jnp.pad(mlstm_gate_b[layer].reshape(1, -1), ((0, 0), (0, LANES - 2 * ML_HEADS)))
        hml = _mlstm(p, g, mlstm_conv_w[layer], mlstm_conv_b[layer].reshape(1, -1), gb,
                     mlstm_norm_w[layer].reshape(1, ML_WIDTH), B, S)

        qw = jnp.tile(q_norm_w[layer], 2).reshape(1, LANES)
        kw = jnp.tile(k_norm_w[layer], 2).reshape(1, LANES)
        qn, kn = _qkprep(positions, p, invf, qw, qw[:, pidx], kw, kw[:, pidx])

        r64 = lambda a: a[layer].reshape(1, DA_DH).astype(F32)
        oda = _attn(qn, kn, p, r64(lambda_q1), r64(lambda_k1), r64(lambda_q2), r64(lambda_k2),
                    subln_w[layer].reshape(1, DA_DV), B, S, lambda_init)

        x1 = _outproj(xf, mod3, hml, oda, w_out[layer].astype(BF16), S)
        xf = _ffn(x1, norm2_w[layer].reshape(1, D), mod3, w_gate_up[layer].astype(BF16),
                  w_down[layer].astype(BF16), S)
    return xf.reshape(B, S, D)
```

```python
import functools
import math

import jax
import jax.numpy as jnp
import numpy as np
from jax import lax
from jax.experimental import pallas as pl
from jax.experimental.pallas import tpu as pltpu

F32 = jnp.float32
BF16 = jnp.bfloat16

D_MODEL = 2048
CHUNK = 64
ML_HEADS = 4
ML_DQK = 128
ML_DV = 256
ML_CONV = 4
GATE_CAP = 15.0
DA_HEADS = 8
DA_DH = 64
DA_DV = 128
ROPE_DIM = 16
ROPE_THETA = 500000.0
ML_WIDTH = ML_HEADS * ML_DV
DA_WIDTH = DA_HEADS * DA_DV
QK_W = ML_HEADS * ML_DQK
D_FF = 5632
EPS = 1e-6
NEG = -1e30
LOG2E = 1.4426950408889634

LANES = 128
VMEM_LIMIT = 52 * 1024 * 1024
VMEM_LIMIT_BIG = 60 * 1024 * 1024

P_COLS = 6144
GATE_LO = 3 * 1024
GATE_HI = GATE_LO + 2 * ML_HEADS

ML_L = 256
AT_T = 512


def _cparams(sem, vmem=VMEM_LIMIT):
    return pltpu.CompilerParams(dimension_semantics=sem, vmem_limit_bytes=vmem)


def _dot(a, b):
    return jnp.dot(a, b, preferred_element_type=F32)


def _dot_nt(a, b):
    return lax.dot_general(a, b, (((1,), (1,)), ((), ())), preferred_element_type=F32)


def _split3(x):
    hi = x.astype(BF16)
    r1 = x - hi.astype(F32)
    mid = r1.astype(BF16)
    lo = (r1 - mid.astype(F32)).astype(BF16)
    return hi, mid, lo


def _ada_kernel(c_ref, w_ref, b_ref, o_ref):
    c = c_ref[...]
    a = (c * jax.nn.sigmoid(c)).astype(BF16)
    o_ref[...] = _dot(a, w_ref[...].astype(BF16)) + b_ref[...]


def _ada(c8, w, b):
    n = w.shape[1]
    tn = 1024
    return pl.pallas_call(
        _ada_kernel,
        out_shape=jax.ShapeDtypeStruct((8, n), F32),
        grid=(n // tn,),
        in_specs=[pl.BlockSpec((8, D_MODEL), lambda j: (0, 0)),
                  pl.BlockSpec((D_MODEL, tn), lambda j: (0, j)),
                  pl.BlockSpec((1, tn), lambda j: (0, j))],
        out_specs=pl.BlockSpec((8, tn), lambda j: (0, j)),
        compiler_params=_cparams(("parallel",)),
        name="adaln",
    )(c8, w, b)


def _norm_mod(x, gain, sh):
    ms = jnp.mean(x * x, axis=-1, keepdims=True)
    return (x * lax.rsqrt(ms + EPS)) * gain + sh


def _norm_mod_rows(x_ref, nw_ref, sc_ref, sh_ref, emit, tm, sub=128):
    gain = nw_ref[...] * (1.0 + sc_ref[...])
    sh = sh_ref[...]

    def body(r, carry):
        off = pl.multiple_of(r * sub, sub)
        emit(off, _norm_mod(x_ref[pl.ds(off, sub), :], gain, sh))
        return carry

    lax.fori_loop(0, tm // sub, body, 0)


IN_TM = 1024
IN_TN = 1024
IN_SUB = 256
IN_NSUB = IN_TM // IN_SUB


def _inproj_kernel(x_ref, nw_ref, sc_ref, sh_ref, w_ref, wg_ref, o_ref, g_ref, ha_ref, hb_ref, *, ni):
    i = pl.program_id(0)
    j = pl.program_id(1)
    hbufs = (ha_ref, hb_ref)

    def norm_slice(dst):
        gain = nw_ref[...] * (1.0 + sc_ref[...])
        hh = _norm_mod(x_ref[...], gain, sh_ref[...]).astype(BF16)
        rows = pl.ds(pl.multiple_of(j * IN_SUB, IN_SUB), IN_SUB)
        dst[rows, :] = hh
        g_ref[rows, :] = _dot_nt(hh, wg_ref[...])

    def matmul(src):
        o_ref[...] = _dot_nt(src[...], w_ref[...]).astype(BF16)

    has_norm = (i < ni) & (j < IN_NSUB)
    has_mm = i > 0

    @pl.when(jnp.logical_not(has_mm) & has_norm)
    def _():
        norm_slice(ha_ref)

    for par in range(2):
        mine = (i % 2) == par

        @pl.when(mine & has_mm & has_norm)
        def _():
            norm_slice(hbufs[par])
            matmul(hbufs[1 - par])

        @pl.when(mine & has_mm & jnp.logical_not(has_norm))
        def _():
            matmul(hbufs[1 - par])


def _inproj(x2, nw, mod3, w_bf, wg, tpb):
    t = x2.shape[0]
    tm, tn = IN_TM, IN_TN
    tiles_per_batch = tpb // tm
    ni = t // tm
    last = ni - 1
    kern = functools.partial(_inproj_kernel, ni=ni)
    mrow = lambda k: pl.BlockSpec(
        (None, 1, D_MODEL), lambda i, j: ((jnp.minimum(i, last) // tiles_per_batch) * 6 + k, 0, 0))
    return pl.pallas_call(
        kern,
        out_shape=(jax.ShapeDtypeStruct((t, P_COLS), BF16), jax.ShapeDtypeStruct((t, LANES), F32)),
        grid=(ni + 1, P_COLS // tn),
        in_specs=[pl.BlockSpec((IN_SUB, D_MODEL),
                               lambda i, j: (jnp.minimum(i, last) * IN_NSUB + jnp.minimum(j, IN_NSUB - 1), 0)),
                  pl.BlockSpec((1, D_MODEL), lambda i, j: (0, 0)),
                  mrow(1), mrow(0),
                  pl.BlockSpec((tn, D_MODEL), lambda i, j: (j, 0)),
                  pl.BlockSpec((LANES, D_MODEL), lambda i, j: (0, 0))],
        out_specs=(pl.BlockSpec((tm, tn), lambda i, j: (jnp.maximum(i - 1, 0), jnp.where(i == 0, 0, j))),
                   pl.BlockSpec((tm, LANES), lambda i, j: (jnp.minimum(i, last), 0))),
        scratch_shapes=[pltpu.VMEM((tm, D_MODEL), BF16), pltpu.VMEM((tm, D_MODEL), BF16)],
        compiler_params=_cparams(("arbitrary", "arbitrary")),
        name="inproj",
    )(x2, nw, mod3, mod3, w_bf, wg)


def _wprep_kernel(w_hbm, o_ref, buf_ref, sem, *, tn, nblk, gate_lo, gate_rows):
    j = pl.program_id(0)

    def window(blk, slot):
        src = pl.multiple_of(blk * tn + jnp.where(blk * tn >= gate_lo, gate_rows, 0), 8)
        return pltpu.make_async_copy(w_hbm.at[pl.ds(src, tn), :], buf_ref.at[slot], sem.at[slot])

    @pl.when(j == 0)
    def _():
        window(0, 0).start()

    @pl.when(j + 1 < nblk)
    def _():
        window(j + 1, (j + 1) % 2).start()

    window(j, j % 2).wait()
    o_ref[...] = buf_ref[j % 2].astype(BF16)


def _wprep(w_t):
    tn = 512
    nblk = P_COLS // tn
    kern = functools.partial(_wprep_kernel, tn=tn, nblk=nblk, gate_lo=GATE_LO, gate_rows=GATE_HI - GATE_LO)
    return pl.pallas_call(
        kern,
        out_shape=jax.ShapeDtypeStruct((P_COLS, D_MODEL), BF16),
        grid=(nblk,),
        in_specs=[pl.BlockSpec(memory_space=pl.ANY)],
        out_specs=pl.BlockSpec((tn, D_MODEL), lambda j: (j, 0)),
        scratch_shapes=[pltpu.VMEM((2, tn, D_MODEL), F32), pltpu.SemaphoreType.DMA((2,))],
        compiler_params=_cparams(("arbitrary",)),
        name="wprep",
    )(w_t)


def _conv_silu(x, tail8, w4, b, row8):
    acc = b + w4[3:4, :] * x
    for k in (1, 2, 3):
        xr = pltpu.roll(x, k, 0)
        tr = pltpu.roll(tail8, k, 0)
        first = jnp.where(row8 < k, tr, xr[0:8, :])
        xs = jnp.concatenate([first, xr[8:, :]], axis=0)
        acc = acc + w4[3 - k:4 - k, :] * xs
    return acc * jax.nn.sigmoid(acc)


def _mlstm_kernel(qk_ref, v_ref, og_ref, g_ref, cw_ref, cb_ref, gb_ref, nw_ref, o_ref, ct_ref, tail_ref):
    L = ML_L
    c = pl.program_id(1)

    @pl.when(c == 0)
    def _():
        ct_ref[...] = jnp.zeros_like(ct_ref)
        tail_ref[...] = jnp.zeros_like(tail_ref)

    g = g_ref[...] + gb_ref[...]
    gc = GATE_CAP * jnp.tanh(g * (1.0 / GATE_CAP))
    lf_all = -jnp.log1p(jnp.exp(-gc))
    ri = lax.broadcasted_iota(jnp.int32, (L, L), 0)
    ci = lax.broadcasted_iota(jnp.int32, (L, L), 1)
    tri = ci <= ri
    trib = jnp.where(tri, 1.0, 0.0).astype(BF16)
    f_hi, f_mid, f_lo = _split3(lf_all)
    b_all = _dot(trib, f_hi) + _dot(trib, f_mid) + _dot(trib, f_lo)
    u_all = gc - pltpu.roll(b_all, LANES - ML_HEADS, 1)
    u_t = u_all.T

    row8 = lax.broadcasted_iota(jnp.int32, (8, LANES), 0)
    ones_b = jnp.ones((L, LANES), BF16)
    tails = tail_ref[...]

    for h in range(ML_HEADS):
        qs = slice(h * ML_DQK, (h + 1) * ML_DQK)
        ks = slice(QK_W + h * ML_DQK, QK_W + (h + 1) * ML_DQK)
        vs = slice(h * ML_DV, (h + 1) * ML_DV)
        q = _conv_silu(qk_ref[:, qs].astype(F32), tails[:, qs], cw_ref[:, qs], cb_ref[:, qs], row8)
        k = _conv_silu(qk_ref[:, ks].astype(F32), tails[:, ks], cw_ref[:, ks], cb_ref[:, ks], row8)
        q = q * (ML_DQK ** -0.5)

        b_col = jnp.broadcast_to(b_all[:, ML_HEADS + h:ML_HEADS + h + 1], (L, LANES))
        u_col = jnp.broadcast_to(u_all[:, h:h + 1], (L, LANES))
        b_end = b_all[L - 1:L, ML_HEADS + h:ML_HEADS + h + 1]
        u_row = u_t[h:h + 1, :]

        logd = jnp.concatenate([b_col, b_col], axis=1) + u_row
        d = jnp.exp(jnp.where(tri, logd, NEG))
        s = lax.dot_general(q.astype(BF16), k.astype(BF16), (((1,), (1,)), ((), ())),
                            preferred_element_type=F32) * d

        vaug = jnp.concatenate([v_ref[:, vs], ones_b], axis=1)
        ct = ct_ref[h]
        tot = _dot((q * jnp.exp(b_col)).astype(BF16), ct.astype(BF16)) + _dot(s.astype(BF16), vaug)
        den = tot[:, ML_DV:]
        den2 = jnp.concatenate([den, den], axis=1)
        hc = tot[:, :ML_DV] / jnp.maximum(jnp.abs(den2), 1.0)

        kw = (k * jnp.exp(b_end + u_col)).astype(BF16)
        upd = lax.dot_general(kw, vaug, (((0,), (0,)), ((), ())), preferred_element_type=F32)
        ct_ref[h] = jnp.exp(b_end) * ct + upd

        ms = jnp.mean(hc * hc, axis=-1, keepdims=True)
        hn = hc * lax.rsqrt(ms + EPS) * nw_ref[:, vs]
        o_ref[:, vs] = (hn * jax.nn.sigmoid(og_ref[:, vs].astype(F32))).astype(BF16)

    tail_ref[...] = qk_ref[L - 16:L, :].astype(F32)[8:16, :]


def _mlstm(p, g, cw, cb, gb, nw, batch, seq):
    t = p.shape[0]
    L = ML_L
    nc = seq // L
    return pl.pallas_call(
        _mlstm_kernel,
        out_shape=jax.ShapeDtypeStruct((t, ML_WIDTH), BF16),
        grid=(batch, nc),
        in_specs=[pl.BlockSpec((L, 1024), lambda b, c: (b * nc + c, 0)),
                  pl.BlockSpec((L, 1024), lambda b, c: (b * nc + c, 1)),
                  pl.BlockSpec((L, 1024), lambda b, c: (b * nc + c, 2)),
                  pl.BlockSpec((L, LANES), lambda b, c: (b * nc + c, 0)),
                  pl.BlockSpec((ML_CONV, 1024), lambda b, c: (0, 0)),
                  pl.BlockSpec((1, 1024), lambda b, c: (0, 0)),
                  pl.BlockSpec((1, LANES), lambda b, c: (0, 0)),
                  pl.BlockSpec((1, ML_WIDTH), lambda b, c: (0, 0))],
        out_specs=pl.BlockSpec((L, ML_WIDTH), lambda b, c: (b * nc + c, 0)),
        scratch_shapes=[pltpu.VMEM((ML_HEADS, ML_DQK, ML_DV + LANES), F32),
                        pltpu.VMEM((8, 1024), F32)],
        compiler_params=_cparams(("parallel", "arbitrary")),
        name="mlstm",
    )(p, p, p, g, cw, cb, gb, nw)


def _qkprep_kernel(pos_ref, q_ref, k_ref, invf_ref, qw_ref, qpw_ref, kw_ref, kpw_ref, qo_ref, ko_ref):
    half = ROPE_DIM // 2
    tm = q_ref.shape[0]
    pos = pos_ref[0].astype(F32)
    ang = jnp.concatenate([invf_ref[...]] * (tm // LANES), axis=1) * pos
    fi = lax.broadcasted_iota(jnp.int32, (4 * half, LANES), 0)
    li = lax.broadcasted_iota(jnp.int32, (4 * half, LANES), 1)
    dl = li & (DA_DH - 1)
    hit = (dl < ROPE_DIM) & ((dl & (half - 1)) == (fi & (half - 1))) & (fi < 3 * half)
    e_cos = jnp.where(hit, 1.0, 0.0).astype(BF16)
    e_sin = jnp.where(hit, jnp.where(dl < half, -1.0, 1.0), 0.0).astype(BF16)

    def expand(t8, e):
        hi, mid, lo = _split3(t8)
        parts = jnp.concatenate([hi.astype(F32), mid.astype(F32), lo.astype(F32), jnp.zeros_like(t8)], axis=0)
        return lax.dot_general(parts.astype(BF16), e, (((0,), (0,)), ((), ())), preferred_element_type=F32)

    lane = lax.broadcasted_iota(jnp.int32, (1, LANES), 1)
    cs = expand(jnp.cos(ang), e_cos) + jnp.where((lane & (DA_DH - 1)) < ROPE_DIM, 0.0, 1.0)
    sn = expand(jnp.sin(ang), e_sin)
    ri = lax.broadcasted_iota(jnp.int32, (LANES, LANES), 0)
    ci = lax.broadcasted_iota(jnp.int32, (LANES, LANES), 1)
    dc = ci & (DA_DH - 1)
    partner = jnp.where(dc < half, ci + half, jnp.where(dc < ROPE_DIM, ci - half, -1))
    perm = jnp.where(ri == partner, 1.0, 0.0).astype(BF16)
    mean_blk = jnp.where((ri >> 6) == (ci >> 6), 1.0 / DA_DH, 0.0).astype(BF16)

    def prep(x_ref, w_ref, pw_ref, o_ref, scale):
        a = (w_ref[...] * scale) * cs
        bc = (pw_ref[...] * scale) * sn
        for hh in range(DA_HEADS):
            cols = slice(hh * LANES, (hh + 1) * LANES)
            xb = x_ref[:, cols]
            xf = xb.astype(F32)
            r = lax.rsqrt(_dot((xf * xf).astype(BF16), mean_blk) + EPS)
            px = _dot(xb, perm)
            o_ref[:, cols] = (r * (xf * a + px * bc)).astype(BF16)

    prep(q_ref, qw_ref, qpw_ref, qo_ref, DA_DH ** -0.5 * LOG2E)
    prep(k_ref, kw_ref, kpw_ref, ko_ref, 1.0)


def _qkprep(positions, p, invf, qw, qpw, kw, kpw):
    t = p.shape[0]
    tm = 512
    pos3 = positions.reshape(t // tm, 1, tm)
    vec = pl.BlockSpec((1, LANES), lambda i: (0, 0))
    return pl.pallas_call(
        _qkprep_kernel,
        out_shape=(jax.ShapeDtypeStruct((t, 1024), BF16), jax.ShapeDtypeStruct((t, 1024), BF16)),
        grid=(t // tm,),
        in_specs=[pl.BlockSpec((1, 1, tm), lambda i: (i, 0, 0)),
                  pl.BlockSpec((tm, 1024), lambda i: (i, 3)),
                  pl.BlockSpec((tm, 1024), lambda i: (i, 4)),
                  pl.BlockSpec((ROPE_DIM // 2, LANES), lambda i: (0, 0)),
                  vec, vec, vec, vec],
        out_specs=(pl.BlockSpec((tm, 1024), lambda i: (i, 0)),
                   pl.BlockSpec((tm, 1024), lambda i: (i, 0))),
        compiler_params=_cparams(("parallel",)),
        name="qkprep",
    )(pos3, p, p, invf, qw, qpw, kw, kpw)


AT_GROUP = 6


def _attn_kernel(q_ref, k_ref, v_ref, lq1_ref, lk1_ref, lq2_ref, lk2_ref, sw_ref, o_ref, m_ref, acc_ref,
                 sa_ref, sb_ref, *, lambda_init, nq):
    T = AT_T
    lane = lax.broadcasted_iota(jnp.int32, (T, LANES), 1)
    ones_b = jnp.ones((T, LANES), BF16)
    sbufs = (sa_ref, sb_ref)
    stages = [(qi, kt) for qi in range(nq) for kt in [qi] + list(range(qi))]

    H = T // 2
    nlt = T // LANES

    def chunk_mask(rows, cols, row0):
        r = lax.broadcasted_iota(jnp.int32, (rows, cols), 0) + row0
        cc = lax.broadcasted_iota(jnp.int32, (rows, cols), 1)
        return (cc >> 6) <= (r >> 6)

    def score(t):
        qi, kt = stages[t]
        q = q_ref[qi * T:(qi + 1) * T, :]
        kb = k_ref[kt * T:(kt + 1) * T, :]
        zero = jnp.zeros_like(q)
        for mi, qm in enumerate((jnp.where(lane < DA_DH, q, zero), jnp.where(lane >= DA_DH, q, zero))):
            if kt == qi:
                sbufs[t % 2][mi, :H, :H] = jnp.where(chunk_mask(H, H, 0), _dot_nt(qm[:H], kb[:H]), NEG)
                sbufs[t % 2][mi, H:, :] = jnp.where(chunk_mask(H, T, H), _dot_nt(qm[H:], kb), NEG)
            else:
                sbufs[t % 2][mi] = _dot_nt(qm, kb)

    def softmax_pv(s, m_old, acc_old, vaug):
        rmax = jnp.max(s, axis=1, keepdims=True)
        ntile = s.shape[1] // LANES
        if m_old is None:
            m_new = jnp.broadcast_to(rmax, (s.shape[0], LANES))
            p = jnp.exp2(s - jnp.concatenate([m_new] * ntile, axis=1)).astype(BF16)
            return m_new, _dot(p, vaug)
        m_new = jnp.maximum(m_old, rmax)
        al = jnp.exp2(m_old - m_new)
        p = jnp.exp2(s - jnp.concatenate([m_new] * ntile, axis=1)).astype(BF16)
        return m_new, acc_old * jnp.concatenate([al, al], axis=1) + _dot(p, vaug)

    def consume(t):
        qi, kt = stages[t]
        vaug = jnp.concatenate([v_ref[kt * T:(kt + 1) * T, :], ones_b], axis=1)
        for mi in range(2):
            if kt == qi:
                m_ref[mi, :H], acc_ref[mi, :H] = softmax_pv(sbufs[t % 2][mi, :H, :H], None, None, vaug[:H])
                m_ref[mi, H:], acc_ref[mi, H:] = softmax_pv(sbufs[t % 2][mi, H:, :], None, None, vaug)
            else:
                m_ref[mi], acc_ref[mi] = softmax_pv(sbufs[t % 2][mi], m_ref[mi], acc_ref[mi], vaug)
        if t + 1 == len(stages) or stages[t + 1][0] != qi:
            finalize(qi)

    def finalize(qi):
        acc0 = acc_ref[0]
        acc1 = acc_ref[1]
        o0 = acc0[:, :DA_DV] / acc0[:, DA_DV:]
        o1 = acc1[:, :DA_DV] / acc1[:, DA_DV:]
        lam = (jnp.exp(jnp.sum(lq1_ref[...] * lk1_ref[...], axis=-1, keepdims=True))
               - jnp.exp(jnp.sum(lq2_ref[...] * lk2_ref[...], axis=-1, keepdims=True)) + lambda_init)
        o = o0 - lam * o1
        ms = jnp.mean(o * o, axis=-1, keepdims=True)
        o_ref[qi * T:(qi + 1) * T, :] = (o * lax.rsqrt(ms + EPS) * (sw_ref[...] * (1.0 - lambda_init))).astype(BF16)

    always = (pl.program_id(0) >= 0, pl.program_id(1) >= 0)
    score(0)
    for g0 in range(0, len(stages), AT_GROUP):
        @pl.when(always[(g0 // AT_GROUP) % 2])
        def _():
            for t in range(g0, min(g0 + AT_GROUP, len(stages))):
                if t + 1 < len(stages):
                    score(t + 1)
                consume(t)


def _attn(qn, kn, p, lq1, lk1, lq2, lk2, sw, batch, seq, lambda_init):
    t = qn.shape[0]
    T = AT_T
    vcol0 = 5 * 1024 // LANES
    vec64 = pl.BlockSpec((1, DA_DH), lambda b, h: (0, 0))
    kern = functools.partial(_attn_kernel, lambda_init=lambda_init, nq=seq // T)
    return pl.pallas_call(
        kern,
        out_shape=jax.ShapeDtypeStruct((t, DA_WIDTH), BF16),
        grid=(batch, DA_HEADS),
        in_specs=[pl.BlockSpec((seq, LANES), lambda b, h: (b, h)),
                  pl.BlockSpec((seq, LANES), lambda b, h: (b, h)),
                  pl.BlockSpec((seq, LANES), lambda b, h: (b, vcol0 + h)),
                  vec64, vec64, vec64, vec64,
                  pl.BlockSpec((1, DA_DV), lambda b, h: (0, 0))],
        out_specs=pl.BlockSpec((seq, LANES), lambda b, h: (b, h)),
        scratch_shapes=[pltpu.VMEM((2, T, LANES), F32), pltpu.VMEM((2, T, 2 * LANES), F32),
                        pltpu.VMEM((2, T, T), F32), pltpu.VMEM((2, T, T), F32)],
        compiler_params=_cparams(("parallel", "parallel")),
        name="diffattn",
    )(qn, kn, p, lq1, lk1, lq2, lk2, sw)


def _outproj_kernel(x_ref, g_ref, a_ref, b_ref, wa_ref, wb_ref, o_ref):
    mix = _dot(a_ref[...], wa_ref[...]) + _dot(b_ref[...], wb_ref[...])
    o_ref[...] = x_ref[...] + g_ref[...] * mix


def _outproj(x2, mod3, hml, oda, w_bf, tpb):
    t = x2.shape[0]
    tm = 512
    tiles_per_batch = tpb // tm
    return pl.pallas_call(
        _outproj_kernel,
        out_shape=jax.ShapeDtypeStruct((t, D_MODEL), F32),
        grid=(t // tm,),
        in_specs=[pl.BlockSpec((tm, D_MODEL), lambda i: (i, 0)),
                  pl.BlockSpec((None, 1, D_MODEL), lambda i: ((i // tiles_per_batch) * 6 + 2, 0, 0)),
                  pl.BlockSpec((tm, ML_WIDTH), lambda i: (i, 0)),
                  pl.BlockSpec((tm, DA_WIDTH), lambda i: (i, 0)),
                  pl.BlockSpec((ML_WIDTH, D_MODEL), lambda i: (0, 0)),
                  pl.BlockSpec((DA_WIDTH, D_MODEL), lambda i: (1, 0))],
        out_specs=pl.BlockSpec((tm, D_MODEL), lambda i: (i, 0)),
        compiler_params=_cparams(("parallel",)),
        name="outproj",
    )(x2, mod3, hml, oda, w_bf, w_bf)


def _ffn_kernel(x_ref, nw_ref, sc_ref, sh_ref, gt_ref, wg_ref, wu_ref, wd_ref, o_ref, h_ref, *, tm, sub, nj):
    j = pl.program_id(1)

    @pl.when(j == 0)
    def _():
        def emit(off, h):
            h_ref[pl.ds(off, sub), :] = h.astype(BF16)
        _norm_mod_rows(x_ref, nw_ref, sc_ref, sh_ref, emit, tm, sub)

    def down():
        h = h_ref[...]
        g = _dot(h, wg_ref[...])
        u = _dot(h, wu_ref[...])
        a = (g * jax.nn.sigmoid(g) * u).astype(BF16)
        return _dot(a, wd_ref[...])

    @pl.when(j == 0)
    def _():
        o_ref[...] = down()

    @pl.when((j > 0) & (j < nj - 1))
    def _():
        o_ref[...] += down()

    @pl.when(j == nj - 1)
    def _():
        o_ref[...] = x_ref[...] + gt_ref[...] * (o_ref[...] + down())


def _ffn(x1, nw, mod3, wgu_bf, wd_bf, tpb):
    t = x1.shape[0]
    tm, tf, sub = 1024, 512, 128
    tiles_per_batch = tpb // tm
    nj = D_FF // tf
    kern = functools.partial(_ffn_kernel, tm=tm, sub=sub, nj=nj)
    mrow = lambda k: pl.BlockSpec((None, 1, D_MODEL), lambda i, j: ((i // tiles_per_batch) * 6 + k, 0, 0))
    return pl.pallas_call(
        kern,
        out_shape=jax.ShapeDtypeStruct((t, D_MODEL), F32),
        grid=(t // tm, nj),
        in_specs=[pl.BlockSpec((tm, D_MODEL), lambda i, j: (i, 0)),
                  pl.BlockSpec((1, D_MODEL), lambda i, j: (0, 0)),
                  mrow(4), mrow(3), mrow(5),
                  pl.BlockSpec((D_MODEL, tf), lambda i, j: (0, j)),
                  pl.BlockSpec((D_MODEL, tf), lambda i, j: (0, nj + j)),
                  pl.BlockSpec((tf, D_MODEL), lambda i, j: (j, 0))],
        out_specs=pl.BlockSpec((tm, D_MODEL), lambda i, j: (i, 0)),
        scratch_shapes=[pltpu.VMEM((tm, D_MODEL), BF16)],
        compiler_params=_cparams(("parallel", "arbitrary"), VMEM_LIMIT_BIG),
        name="ffn",
    )(x1, nw, mod3, mod3, mod3, wgu_bf, wgu_bf, wd_bf)


def _rope_partner_index():
    idx = np.arange(LANES)
    d = idx % DA_DH
    half = ROPE_DIM // 2
    return np.where(d < half, idx + half, np.where(d < ROPE_DIM, idx - half, idx))


def kernel(x, c, positions, norm1_w, norm2_w, w_ada, b_ada, w_in, mlstm_conv_w, mlstm_conv_b, mlstm_gate_b,
           mlstm_norm_w, q_norm_w, k_norm_w, lambda_q1, lambda_k1, lambda_q2, lambda_k2, subln_w, w_out,
           w_gate_up, w_down):
    B, S, D = x.shape
    T = B * S
    depth = w_in.shape[0]
    xf = x.reshape(T, D)
    c8 = jnp.pad(c, ((0, 8 - B), (0, 0)))

    inv_freq = ROPE_THETA ** (-jnp.arange(0, ROPE_DIM, 2, dtype=F32) / ROPE_DIM)
    invf = jnp.broadcast_to(inv_freq[:, None], (ROPE_DIM // 2, LANES))
    pidx = _rope_partner_index()

    for layer in range(depth):
        lambda_init = 0.8 - 0.6 * math.exp(-0.3 * layer)
        mod = _ada(c8, w_ada[layer], b_ada[layer].reshape(1, -1))
        mod3 = mod[:B].reshape(B * 6, 1, D)

        w_l = w_in[layer]
        w_t = w_l.T
        w_bf = _wprep(w_t)
        wgate = jnp.pad(w_t[GATE_LO:GATE_HI], ((0, LANES - 2 * ML_HEADS), (0, 0)))
        wg = wgate.astype(BF16)

        p, g = _inproj(xf, norm1_w[layer].reshape(1, D), mod3, w_bf, wg, S)

        gb = jnp.pad(mlstm_gate_b[layer].reshape(1, -1), ((0, 0), (0, LANES - 2 * ML_HEADS)))
        hml = _mlstm(p, g, mlstm_conv_w[layer], mlstm_conv_b[layer].reshape(1, -1), gb,
                     mlstm_norm_w[layer].reshape(1, ML_WIDTH), B, S)

        qw = jnp.tile(q_norm_w[layer], 2).reshape(1, LANES)
        kw = jnp.tile(k_norm_w[layer], 2).reshape(1, LANES)
        qn, kn = _qkprep(positions, p, invf, qw, qw[:, pidx], kw, kw[:, pidx])

        r64 = lambda a: a[layer].reshape(1, DA_DH).astype(F32)
        oda = _attn(qn, kn, p, r64(lambda_q1), r64(lambda_k1), r64(lambda_q2), r64(lambda_k2),
                    subln_w[layer].reshape(1, DA_DV), B, S, lambda_init)

        x1 = _outproj(xf, mod3, hml, oda, w_out[layer].astype(BF16), S)
        xf = _ffn(x1, norm2_w[layer].reshape(1, D), mod3, w_gate_up[layer].astype(BF16),
                  w_down[layer].astype(BF16), S)
    return xf.reshape(B, S, D)
```

```python
import functools
import math

import jax
import jax.numpy as jnp
import numpy as np
from jax import lax
from jax.experimental import pallas as pl
from jax.experimental.pallas import tpu as pltpu

F32 = jnp.float32
BF16 = jnp.bfloat16

D_MODEL = 2048
CHUNK = 64
ML_HEADS = 4
ML_DQK = 128
ML_DV = 256
ML_CONV = 4
GATE_CAP = 15.0
DA_HEADS = 8
DA_DH = 64
DA_DV = 128
ROPE_DIM = 16
ROPE_THETA = 500000.0
ML_WIDTH = ML_HEADS * ML_DV
DA_WIDTH = DA_HEADS * DA_DV
QK_W = ML_HEADS * ML_DQK
D_FF = 5632
EPS = 1e-6
NEG = -1e30
LOG2E = 1.4426950408889634

LANES = 128
VMEM_LIMIT = 52 * 1024 * 1024
VMEM_LIMIT_BIG = 60 * 1024 * 1024

P_COLS = 6144
GATE_LO = 3 * 1024
GATE_HI = GATE_LO + 2 * ML_HEADS

ML_L = 256
AT_T = 512


def _cparams(sem, vmem=VMEM_LIMIT):
    return pltpu.CompilerParams(dimension_semantics=sem, vmem_limit_bytes=vmem)


def _dot(a, b):
    return jnp.dot(a, b, preferred_element_type=F32)


def _dot_nt(a, b):
    return lax.dot_general(a, b, (((1,), (1,)), ((), ())), preferred_element_type=F32)


def _split3(x):
    hi = x.astype(BF16)
    r1 = x - hi.astype(F32)
    mid = r1.astype(BF16)
    lo = (r1 - mid.astype(F32)).astype(BF16)
    return hi, mid, lo


def _ada_kernel(c_ref, w_ref, b_ref, o_ref):
    c = c_ref[...]
    a = (c * jax.nn.sigmoid(c)).astype(BF16)
    o_ref[...] = _dot(a, w_ref[...].astype(BF16)) + b_ref[...]


def _ada(c8, w, b):
    n = w.shape[1]
    tn = 1024
    return pl.pallas_call(
        _ada_kernel,
        out_shape=jax.ShapeDtypeStruct((8, n), F32),
        grid=(n // tn,),
        in_specs=[pl.BlockSpec((8, D_MODEL), lambda j: (0, 0)),
                  pl.BlockSpec((D_MODEL, tn), lambda j: (0, j)),
                  pl.BlockSpec((1, tn), lambda j: (0, j))],
        out_specs=pl.BlockSpec((8, tn), lambda j: (0, j)),
        compiler_params=_cparams(("parallel",)),
        name="adaln",
    )(c8, w, b)


def _norm_mod(x, gain, sh):
    ms = jnp.mean(x * x, axis=-1, keepdims=True)
    return (x * lax.rsqrt(ms + EPS)) * gain + sh


def _norm_mod_rows(x_ref, nw_ref, sc_ref, sh_ref, emit, tm, sub=128):
    gain = nw_ref[...] * (1.0 + sc_ref[...])
    sh = sh_ref[...]

    def body(r, carry):
        off = pl.multiple_of(r * sub, sub)
        emit(off, _norm_mod(x_ref[pl.ds(off, sub), :], gain, sh))
        return carry

    lax.fori_loop(0, tm // sub, body, 0)


IN_TM = 1024
IN_TN = 1024
IN_SUB = 256
IN_NSUB = IN_TM // IN_SUB


def _inproj_kernel(x_ref, nw_ref, sc_ref, sh_ref, w_ref, wg_ref, gu_ref, dn_ref, wo_ref, o_ref, g_ref, gub_ref, dnb_ref,
                   wob_ref, ha_ref, hb_ref, *, ni):
    i = pl.program_id(0)
    j = pl.program_id(1)
    hbufs = (ha_ref, hb_ref)

    def cast_slab():
        gub_ref[...] = gu_ref[...].astype(BF16)
        dnb_ref[...] = dn_ref[...].astype(BF16)
        wob_ref[...] = wo_ref[...].astype(BF16)

    def norm_slice(dst):
        gain = nw_ref[...] * (1.0 + sc_ref[...])
        hh = _norm_mod(x_ref[...], gain, sh_ref[...]).astype(BF16)
        rows = pl.ds(pl.multiple_of(j * IN_SUB, IN_SUB), IN_SUB)
        dst[rows, :] = hh
        g_ref[rows, :] = _dot_nt(hh, wg_ref[...])

    def matmul(src):
        o_ref[...] = _dot_nt(src[...], w_ref[...]).astype(BF16)

    has_norm = (i < ni) & (j < IN_NSUB)
    has_mm = i > 0

    @pl.when(jnp.logical_not(has_mm) & has_norm)
    def _():
        cast_slab()
        norm_slice(ha_ref)

    @pl.when(jnp.logical_not(has_mm) & jnp.logical_not(has_norm))
    def _():
        cast_slab()

    for par in range(2):
        mine = (i % 2) == par

        @pl.when(mine & has_mm & has_norm)
        def _():
            cast_slab()
            norm_slice(hbufs[par])
            matmul(hbufs[1 - par])

        @pl.when(mine & has_mm & jnp.logical_not(has_norm))
        def _():
            cast_slab()
            matmul(hbufs[1 - par])


def _inproj(x2, nw, mod3, w_bf, wg, w_gu, w_dn, w_o, tpb):
    t = x2.shape[0]
    tm, tn = IN_TM, IN_TN
    tiles_per_batch = tpb // tm
    ni = t // tm
    nj = P_COLS // tn
    last = ni - 1
    nslab = w_gu.shape[1] // LANES
    dn_rows = w_dn.shape[0] // nslab
    wo_rows = 32
    nslab_o = w_o.shape[0] // wo_rows
    assert max(nslab, nslab_o) <= ni * nj and dn_rows * nslab == w_dn.shape[0] and dn_rows % 16 == 0
    slab = lambda i, j: jnp.minimum(i * nj + j, nslab - 1)
    slab_o = lambda i, j: jnp.minimum(i * nj + j, nslab_o - 1)
    kern = functools.partial(_inproj_kernel, ni=ni)
    mrow = lambda k: pl.BlockSpec(
        (None, 1, D_MODEL), lambda i, j: ((jnp.minimum(i, last) // tiles_per_batch) * 6 + k, 0, 0))
    return pl.pallas_call(
        kern,
        out_shape=(jax.ShapeDtypeStruct((t, P_COLS), BF16), jax.ShapeDtypeStruct((t, LANES), F32),
                   jax.ShapeDtypeStruct(w_gu.shape, BF16), jax.ShapeDtypeStruct(w_dn.shape, BF16),
                   jax.ShapeDtypeStruct(w_o.shape, BF16)),
        grid=(ni + 1, nj),
        in_specs=[pl.BlockSpec((IN_SUB, D_MODEL),
                               lambda i, j: (jnp.minimum(i, last) * IN_NSUB + jnp.minimum(j, IN_NSUB - 1), 0)),
                  pl.BlockSpec((1, D_MODEL), lambda i, j: (0, 0)),
                  mrow(1), mrow(0),
                  pl.BlockSpec((tn, D_MODEL), lambda i, j: (j, 0)),
                  pl.BlockSpec((LANES, D_MODEL), lambda i, j: (0, 0)),
                  pl.BlockSpec((D_MODEL, LANES), lambda i, j: (0, slab(i, j))),
                  pl.BlockSpec((dn_rows, D_MODEL), lambda i, j: (slab(i, j), 0)),
                  pl.BlockSpec((wo_rows, D_MODEL), lambda i, j: (slab_o(i, j), 0))],
        out_specs=(pl.BlockSpec((tm, tn), lambda i, j: (jnp.maximum(i - 1, 0), jnp.where(i == 0, 0, j))),
                   pl.BlockSpec((tm, LANES), lambda i, j: (jnp.minimum(i, last), 0)),
                   pl.BlockSpec((D_MODEL, LANES), lambda i, j: (0, slab(i, j))),
                   pl.BlockSpec((dn_rows, D_MODEL), lambda i, j: (slab(i, j), 0)),
                   pl.BlockSpec((wo_rows, D_MODEL), lambda i, j: (slab_o(i, j), 0))),
        scratch_shapes=[pltpu.VMEM((tm, D_MODEL), BF16), pltpu.VMEM((tm, D_MODEL), BF16)],
        compiler_params=_cparams(("arbitrary", "arbitrary")),
        name="inproj",
    )(x2, nw, mod3, mod3, w_bf, wg, w_gu, w_dn, w_o)


def _wprep_kernel(w_hbm, o_ref, buf_ref, sem, *, tn, nblk, gate_lo, gate_rows):
    j = pl.program_id(0)

    def window(blk, slot):
        src = pl.multiple_of(blk * tn + jnp.where(blk * tn >= gate_lo, gate_rows, 0), 8)
        return pltpu.make_async_copy(w_hbm.at[pl.ds(src, tn), :], buf_ref.at[slot], sem.at[slot])

    @pl.when(j == 0)
    def _():
        window(0, 0).start()

    @pl.when(j + 1 < nblk)
    def _():
        window(j + 1, (j + 1) % 2).start()

    window(j, j % 2).wait()
    o_ref[...] = buf_ref[j % 2].astype(BF16)


def _wprep(w_t):
    tn = 512
    nblk = P_COLS // tn
    kern = functools.partial(_wprep_kernel, tn=tn, nblk=nblk, gate_lo=GATE_LO, gate_rows=GATE_HI - GATE_LO)
    return pl.pallas_call(
        kern,
        out_shape=jax.ShapeDtypeStruct((P_COLS, D_MODEL), BF16),
        grid=(nblk,),
        in_specs=[pl.BlockSpec(memory_space=pl.ANY)],
        out_specs=pl.BlockSpec((tn, D_MODEL), lambda j: (j, 0)),
        scratch_shapes=[pltpu.VMEM((2, tn, D_MODEL), F32), pltpu.SemaphoreType.DMA((2,))],
        compiler_params=_cparams(("arbitrary",)),
        name="wprep",
    )(w_t)


def _conv_silu(x, tail8, w4, b, row8):
    acc = b + w4[3:4, :] * x
    for k in (1, 2, 3):
        xr = pltpu.roll(x, k, 0)
        tr = pltpu.roll(tail8, k, 0)
        first = jnp.where(row8 < k, tr, xr[0:8, :])
        xs = jnp.concatenate([first, xr[8:, :]], axis=0)
        acc = acc + w4[3 - k:4 - k, :] * xs
    return acc * jax.nn.sigmoid(acc)


def _mlstm_kernel(qk_ref, v_ref, og_ref, g_ref, cw_ref, cb_ref, gb_ref, nw_ref, o_ref, ct_ref, tail_ref):
    L = ML_L
    c = pl.program_id(1)

    @pl.when(c == 0)
    def _():
        ct_ref[...] = jnp.zeros_like(ct_ref)
        tail_ref[...] = jnp.zeros_like(tail_ref)

    g = g_ref[...] + gb_ref[...]
    gc = GATE_CAP * jnp.tanh(g * (1.0 / GATE_CAP))
    lf_all = -jnp.log1p(jnp.exp(-gc))
    ri = lax.broadcasted_iota(jnp.int32, (L, L), 0)
    ci = lax.broadcasted_iota(jnp.int32, (L, L), 1)
    tri = ci <= ri
    trib = jnp.where(tri, 1.0, 0.0).astype(BF16)
    f_hi, f_mid, f_lo = _split3(lf_all)
    b_all = _dot(trib, f_hi) + _dot(trib, f_mid) + _dot(trib, f_lo)
    u_all = gc - pltpu.roll(b_all, LANES - ML_HEADS, 1)
    u_t = u_all.T

    row8 = lax.broadcasted_iota(jnp.int32, (8, LANES), 0)
    ones_b = jnp.ones((L, LANES), BF16)
    tails = tail_ref[...]

    for h in range(ML_HEADS):
        qs = slice(h * ML_DQK, (h + 1) * ML_DQK)
        ks = slice(QK_W + h * ML_DQK, QK_W + (h + 1) * ML_DQK)
        vs = slice(h * ML_DV, (h + 1) * ML_DV)
        q = _conv_silu(qk_ref[:, qs].astype(F32), tails[:, qs], cw_ref[:, qs], cb_ref[:, qs], row8)
        k = _conv_silu(qk_ref[:, ks].astype(F32), tails[:, ks], cw_ref[:, ks], cb_ref[:, ks], row8)
        q = q * (ML_DQK ** -0.5)

        b_col = jnp.broadcast_to(b_all[:, ML_HEADS + h:ML_HEADS + h + 1], (L, LANES))
        u_col = jnp.broadcast_to(u_all[:, h:h + 1], (L, LANES))
        b_end = b_all[L - 1:L, ML_HEADS + h:ML_HEADS + h + 1]
        u_row = u_t[h:h + 1, :]

        logd = jnp.concatenate([b_col, b_col], axis=1) + u_row
        d = jnp.exp(jnp.where(tri, logd, NEG))
        s = lax.dot_general(q.astype(BF16), k.astype(BF16), (((1,), (1,)), ((), ())),
                            preferred_element_type=F32) * d

        vaug = jnp.concatenate([v_ref[:, vs], ones_b], axis=1)
        ct = ct_ref[h]
        tot = _dot((q * jnp.exp(b_col)).astype(BF16), ct.astype(BF16)) + _dot(s.astype(BF16), vaug)
        den = tot[:, ML_DV:]
        den2 = jnp.concatenate([den, den], axis=1)
        hc = tot[:, :ML_DV] / jnp.maximum(jnp.abs(den2), 1.0)

        kw = (k * jnp.exp(b_end + u_col)).astype(BF16)
        upd = lax.dot_general(kw, vaug, (((0,), (0,)), ((), ())), preferred_element_type=F32)
        ct_ref[h] = jnp.exp(b_end) * ct + upd

        ms = jnp.mean(hc * hc, axis=-1, keepdims=True)
        hn = hc * lax.rsqrt(ms + EPS) * nw_ref[:, vs]
        o_ref[:, vs] = (hn * jax.nn.sigmoid(og_ref[:, vs].astype(F32))).astype(BF16)

    tail_ref[...] = qk_ref[L - 16:L, :].astype(F32)[8:16, :]


def _mlstm(p, g, cw, cb, gb, nw, batch, seq):
    t = p.shape[0]
    L = ML_L
    nc = seq // L
    return pl.pallas_call(
        _mlstm_kernel,
        out_shape=jax.ShapeDtypeStruct((t, ML_WIDTH), BF16),
        grid=(batch, nc),
        in_specs=[pl.BlockSpec((L, 1024), lambda b, c: (b * nc + c, 0)),
                  pl.BlockSpec((L, 1024), lambda b, c: (b * nc + c, 1)),
                  pl.BlockSpec((L, 1024), lambda b, c: (b * nc + c, 2)),
                  pl.BlockSpec((L, LANES), lambda b, c: (b * nc + c, 0)),
                  pl.BlockSpec((ML_CONV, 1024), lambda b, c: (0, 0)),
                  pl.BlockSpec((1, 1024), lambda b, c: (0, 0)),
                  pl.BlockSpec((1, LANES), lambda b, c: (0, 0)),
                  pl.BlockSpec((1, ML_WIDTH), lambda b, c: (0, 0))],
        out_specs=pl.BlockSpec((L, ML_WIDTH), lambda b, c: (b * nc + c, 0)),
        scratch_shapes=[pltpu.VMEM((ML_HEADS, ML_DQK, ML_DV + LANES), F32),
                        pltpu.VMEM((8, 1024), F32)],
        compiler_params=_cparams(("parallel", "arbitrary")),
        name="mlstm",
    )(p, p, p, g, cw, cb, gb, nw)


def _qkprep_kernel(pos_ref, q_ref, k_ref, invf_ref, qw_ref, qpw_ref, kw_ref, kpw_ref, qo_ref, ko_ref):
    half = ROPE_DIM // 2
    tm = q_ref.shape[0]
    pos = pos_ref[0].astype(F32)
    ang = jnp.concatenate([invf_ref[...]] * (tm // LANES), axis=1) * pos
    fi = lax.broadcasted_iota(jnp.int32, (4 * half, LANES), 0)
    li = lax.broadcasted_iota(jnp.int32, (4 * half, LANES), 1)
    dl = li & (DA_DH - 1)
    hit = (dl < ROPE_DIM) & ((dl & (half - 1)) == (fi & (half - 1))) & (fi < 3 * half)
    e_cos = jnp.where(hit, 1.0, 0.0).astype(BF16)
    e_sin = jnp.where(hit, jnp.where(dl < half, -1.0, 1.0), 0.0).astype(BF16)

    def expand(t8, e):
        hi, mid, lo = _split3(t8)
        parts = jnp.concatenate([hi.astype(F32), mid.astype(F32), lo.astype(F32), jnp.zeros_like(t8)], axis=0)
        return lax.dot_general(parts.astype(BF16), e, (((0,), (0,)), ((), ())), preferred_element_type=F32)

    lane = lax.broadcasted_iota(jnp.int32, (1, LANES), 1)
    cs = expand(jnp.cos(ang), e_cos) + jnp.where((lane & (DA_DH - 1)) < ROPE_DIM, 0.0, 1.0)
    sn = expand(jnp.sin(ang), e_sin)
    ri = lax.broadcasted_iota(jnp.int32, (LANES, LANES), 0)
    ci = lax.broadcasted_iota(jnp.int32, (LANES, LANES), 1)
    dc = ci & (DA_DH - 1)
    partner = jnp.where(dc < half, ci + half, jnp.where(dc < ROPE_DIM, ci - half, -1))
    perm = jnp.where(ri == partner, 1.0, 0.0).astype(BF16)
    mean_blk = jnp.where((ri >> 6) == (ci >> 6), 1.0 / DA_DH, 0.0).astype(BF16)

    def prep(x_ref, w_ref, pw_ref, o_ref, scale):
        a = (w_ref[...] * scale) * cs
        bc = (pw_ref[...] * scale) * sn
        for hh in range(DA_HEADS):
            cols = slice(hh * LANES, (hh + 1) * LANES)
            xb = x_ref[:, cols]
            xf = xb.astype(F32)
            r = lax.rsqrt(_dot((xf * xf).astype(BF16), mean_blk) + EPS)
            px = _dot(xb, perm)
            o_ref[:, cols] = (r * (xf * a + px * bc)).astype(BF16)

    prep(q_ref, qw_ref, qpw_ref, qo_ref, DA_DH ** -0.5 * LOG2E)
    prep(k_ref, kw_ref, kpw_ref, ko_ref, 1.0)


def _qkprep(positions, p, invf, qw, qpw, kw, kpw):
    t = p.shape[0]
    tm = 512
    pos3 = positions.reshape(t // tm, 1, tm)
    vec = pl.BlockSpec((1, LANES), lambda i: (0, 0))
    return pl.pallas_call(
        _qkprep_kernel,
        out_shape=(jax.ShapeDtypeStruct((t, 1024), BF16), jax.ShapeDtypeStruct((t, 1024), BF16)),
        grid=(t // tm,),
        in_specs=[pl.BlockSpec((1, 1, tm), lambda i: (i, 0, 0)),
                  pl.BlockSpec((tm, 1024), lambda i: (i, 3)),
                  pl.BlockSpec((tm, 1024), lambda i: (i, 4)),
                  pl.BlockSpec((ROPE_DIM // 2, LANES), lambda i: (0, 0)),
                  vec, vec, vec, vec],
        out_specs=(pl.BlockSpec((tm, 1024), lambda i: (i, 0)),
                   pl.BlockSpec((tm, 1024), lambda i: (i, 0))),
        compiler_params=_cparams(("parallel",)),
        name="qkprep",
    )(pos3, p, p, invf, qw, qpw, kw, kpw)


AT_GROUP = 6


def _attn_kernel(q_ref, k_ref, v_ref, lq1_ref, lk1_ref, lq2_ref, lk2_ref, sw_ref, o_ref, m_ref, acc_ref,
                 sa_ref, sb_ref, *, lambda_init, nq):
    T = AT_T
    lane = lax.broadcasted_iota(jnp.int32, (T, LANES), 1)
    ones_b = jnp.ones((T, LANES), BF16)
    sbufs = (sa_ref, sb_ref)
    stages = [(qi, kt) for qi in range(nq) for kt in [qi] + list(range(qi))]

    H = T // 2
    nlt = T // LANES

    def chunk_mask(rows, cols, row0):
        r = lax.broadcasted_iota(jnp.int32, (rows, cols), 0) + row0
        cc = lax.broadcasted_iota(jnp.int32, (rows, cols), 1)
        return (cc >> 6) <= (r >> 6)

    def score(t):
        qi, kt = stages[t]
        q = q_ref[qi * T:(qi + 1) * T, :]
        kb = k_ref[kt * T:(kt + 1) * T, :]
        zero = jnp.zeros_like(q)
        for mi, qm in enumerate((jnp.where(lane < DA_DH, q, zero), jnp.where(lane >= DA_DH, q, zero))):
            if kt == qi:
                sbufs[t % 2][mi, :H, :H] = jnp.where(chunk_mask(H, H, 0), _dot_nt(qm[:H], kb[:H]), NEG)
                sbufs[t % 2][mi, H:, :] = jnp.where(chunk_mask(H, T, H), _dot_nt(qm[H:], kb), NEG)
            else:
                sbufs[t % 2][mi] = _dot_nt(qm, kb)

    def softmax_pv(s, m_old, acc_old, vaug):
        rmax = jnp.max(s, axis=1, keepdims=True)
        ntile = s.shape[1] // LANES
        if m_old is None:
            m_new = jnp.broadcast_to(rmax, (s.shape[0], LANES))
            p = jnp.exp2(s - jnp.concatenate([m_new] * ntile, axis=1)).astype(BF16)
            return m_new, _dot(p, vaug)
        m_new = jnp.maximum(m_old, rmax)
        al = jnp.exp2(m_old - m_new)
        p = jnp.exp2(s - jnp.concatenate([m_new] * ntile, axis=1)).astype(BF16)
        return m_new, acc_old * jnp.concatenate([al, al], axis=1) + _dot(p, vaug)

    def consume(t):
        qi, kt = stages[t]
        vaug = jnp.concatenate([v_ref[kt * T:(kt + 1) * T, :], ones_b], axis=1)
        for mi in range(2):
            if kt == qi:
                m_ref[mi, :H], acc_ref[mi, :H] = softmax_pv(sbufs[t % 2][mi, :H, :H], None, None, vaug[:H])
                m_ref[mi, H:], acc_ref[mi, H:] = softmax_pv(sbufs[t % 2][mi, H:, :], None, None, vaug)
            else:
                m_ref[mi], acc_ref[mi] = softmax_pv(sbufs[t % 2][mi], m_ref[mi], acc_ref[mi], vaug)
        if t + 1 == len(stages) or stages[t + 1][0] != qi:
            finalize(qi)

    def finalize(qi):
        acc0 = acc_ref[0]
        acc1 = acc_ref[1]
        o0 = acc0[:, :DA_DV] / acc0[:, DA_DV:]
        o1 = acc1[:, :DA_DV] / acc1[:, DA_DV:]
        lam = (jnp.exp(jnp.sum(lq1_ref[...] * lk1_ref[...], axis=-1, keepdims=True))
               - jnp.exp(jnp.sum(lq2_ref[...] * lk2_ref[...], axis=-1, keepdims=True)) + lambda_init)
        o = o0 - lam * o1
        ms = jnp.mean(o * o, axis=-1, keepdims=True)
        o_ref[qi * T:(qi + 1) * T, :] = (o * lax.rsqrt(ms + EPS) * (sw_ref[...] * (1.0 - lambda_init))).astype(BF16)

    always = (pl.program_id(0) >= 0, pl.program_id(1) >= 0)
    score(0)
    for g0 in range(0, len(stages), AT_GROUP):
        @pl.when(always[(g0 // AT_GROUP) % 2])
        def _():
            for t in range(g0, min(g0 + AT_GROUP, len(stages))):
                if t + 1 < len(stages):
                    score(t + 1)
                consume(t)


def _attn(qn, kn, p, lq1, lk1, lq2, lk2, sw, batch, seq, lambda_init):
    t = qn.shape[0]
    T = AT_T
    vcol0 = 5 * 1024 // LANES
    vec64 = pl.BlockSpec((1, DA_DH), lambda b, h: (0, 0))
    kern = functools.partial(_attn_kernel, lambda_init=lambda_init, nq=seq // T)
    return pl.pallas_call(
        kern,
        out_shape=jax.ShapeDtypeStruct((t, DA_WIDTH), BF16),
        grid=(batch, DA_HEADS),
        in_specs=[pl.BlockSpec((seq, LANES), lambda b, h: (b, h)),
                  pl.BlockSpec((seq, LANES), lambda b, h: (b, h)),
                  pl.BlockSpec((seq, LANES), lambda b, h: (b, vcol0 + h)),
                  vec64, vec64, vec64, vec64,
                  pl.BlockSpec((1, DA_DV), lambda b, h: (0, 0))],
        out_specs=pl.BlockSpec((seq, LANES), lambda b, h: (b, h)),
        scratch_shapes=[pltpu.VMEM((2, T, LANES), F32), pltpu.VMEM((2, T, 2 * LANES), F32),
                        pltpu.VMEM((2, T, T), F32), pltpu.VMEM((2, T, T), F32)],
        compiler_params=_cparams(("parallel", "parallel")),
        name="diffattn",
    )(qn, kn, p, lq1, lk1, lq2, lk2, sw)


def _outproj_kernel(x_ref, g_ref, a_ref, b_ref, wa_ref, wb_ref, o_ref):
    mix = _dot(a_ref[...], wa_ref[...]) + _dot(b_ref[...], wb_ref[...])
    o_ref[...] = x_ref[...] + g_ref[...] * mix


def _outproj(x2, mod3, hml, oda, w_bf, tpb):
    t = x2.shape[0]
    tm = 512
    tiles_per_batch = tpb // tm
    return pl.pallas_call(
        _outproj_kernel,
        out_shape=jax.ShapeDtypeStruct((t, D_MODEL), F32),
        grid=(t // tm,),
        in_specs=[pl.BlockSpec((tm, D_MODEL), lambda i: (i, 0)),
                  pl.BlockSpec((None, 1, D_MODEL), lambda i: ((i // tiles_per_batch) * 6 + 2, 0, 0)),
                  pl.BlockSpec((tm, ML_WIDTH), lambda i: (i, 0)),
                  pl.BlockSpec((tm, DA_WIDTH), lambda i: (i, 0)),
                  pl.BlockSpec((ML_WIDTH, D_MODEL), lambda i: (0, 0)),
                  pl.BlockSpec((DA_WIDTH, D_MODEL), lambda i: (1, 0))],
        out_specs=pl.BlockSpec((tm, D_MODEL), lambda i: (i, 0)),
        compiler_params=_cparams(("parallel",)),
        name="outproj",
    )(x2, mod3, hml, oda, w_bf, w_bf)


def _ffn_kernel(x_ref, nw_ref, sc_ref, sh_ref, gt_ref, wg_ref, wu_ref, wd_ref, o_ref, h_ref, *, tm, sub, nj):
    j = pl.program_id(1)

    @pl.when(j == 0)
    def _():
        def emit(off, h):
            h_ref[pl.ds(off, sub), :] = h.astype(BF16)
        _norm_mod_rows(x_ref, nw_ref, sc_ref, sh_ref, emit, tm, sub)

    def down():
        h = h_ref[...]
        g = _dot(h, wg_ref[...])
        u = _dot(h, wu_ref[...])
        a = (g * jax.nn.sigmoid(g) * u).astype(BF16)
        return _dot(a, wd_ref[...])

    @pl.when(j == 0)
    def _():
        o_ref[...] = down()

    @pl.when((j > 0) & (j < nj - 1))
    def _():
        o_ref[...] += down()

    @pl.when(j == nj - 1)
    def _():
        o_ref[...] = x_ref[...] + gt_ref[...] * (o_ref[...] + down())


def _ffn(x1, nw, mod3, wgu_bf, wd_bf, tpb):
    t = x1.shape[0]
    tm, tf, sub = 1024, 512, 128
    tiles_per_batch = tpb // tm
    nj = D_FF // tf
    kern = functools.partial(_ffn_kernel, tm=tm, sub=sub, nj=nj)
    mrow = lambda k: pl.BlockSpec((None, 1, D_MODEL), lambda i, j: ((i // tiles_per_batch) * 6 + k, 0, 0))
    return pl.pallas_call(
        kern,
        out_shape=jax.ShapeDtypeStruct((t, D_MODEL), F32),
        grid=(t // tm, nj),
        in_specs=[pl.BlockSpec((tm, D_MODEL), lambda i, j: (i, 0)),
                  pl.BlockSpec((1, D_MODEL), lambda i, j: (0, 0)),
                  mrow(4), mrow(3), mrow(5),
                  pl.BlockSpec((D_MODEL, tf), lambda i, j: (0, j)),
                  pl.BlockSpec((D_MODEL, tf), lambda i, j: (0, nj + j)),
                  pl.BlockSpec((tf, D_MODEL), lambda i, j: (j, 0))],
        out_specs=pl.BlockSpec((tm, D_MODEL), lambda i, j: (i, 0)),
        scratch_shapes=[pltpu.VMEM((tm, D_MODEL), BF16)],
        compiler_params=_cparams(("parallel", "arbitrary"), VMEM_LIMIT_BIG),
        name="ffn",
    )(x1, nw, mod3, mod3, mod3, wgu_bf, wgu_bf, wd_bf)


def _rope_partner_index():
    idx = np.arange(LANES)
    d = idx % DA_DH
    half = ROPE_DIM // 2
    return np.where(d < half, idx + half, np.where(d < ROPE_DIM, idx - half, idx))


def kernel(x, c, positions, norm1_w, norm2_w, w_ada, b_ada, w_in, mlstm_conv_w, mlstm_conv_b, mlstm_gate_b,
           mlstm_norm_w, q_norm_w, k_norm_w, lambda_q1, lambda_k1, lambda_q2, lambda_k2, subln_w, w_out,
           w_gate_up, w_down):
    B, S, D = x.shape
    T = B * S
    depth = w_in.shape[0]
    xf = x.reshape(T, D)
    c8 = jnp.pad(c, ((0, 8 - B), (0, 0)))

    inv_freq = ROPE_THETA ** (-jnp.arange(0, ROPE_DIM, 2, dtype=F32) / ROPE_DIM)
    invf = jnp.broadcast_to(inv_freq[:, None], (ROPE_DIM // 2, LANES))
    pidx = _rope_partner_index()

    for layer in range(depth):
        lambda_init = 0.8 - 0.6 * math.exp(-0.3 * layer)
        mod = _ada(c8, w_ada[layer], b_ada[layer].reshape(1, -1))
        mod3 = mod[:B].reshape(B * 6, 1, D)

        w_l = w_in[layer]
        w_t = w_l.T
        w_bf = _wprep(w_t)
        wgate = jnp.pad(w_t[GATE_LO:GATE_HI], ((0, LANES - 2 * ML_HEADS), (0, 0)))
        wg = wgate.astype(BF16)

        p, g, wgu_bf, wdn_bf, wo_bf = _inproj(xf, norm1_w[layer].reshape(1, D), mod3, w_bf, wg,
                                              w_gate_up[layer], w_down[layer], w_out[layer], S)

        gb = jnp.pad(mlstm_gate_b[layer].reshape(1, -1), ((0, 0), (0, LANES - 2 * ML_HEADS)))
        hml = _mlstm(p, g, mlstm_conv_w[layer], mlstm_conv_b[layer].reshape(1, -1), gb,
                     mlstm_norm_w[layer].reshape(1, ML_WIDTH), B, S)

        qw = jnp.tile(q_norm_w[layer], 2).reshape(1, LANES)
        kw = jnp.tile(k_norm_w[layer], 2).reshape(1, LANES)
        qn, kn = _qkprep(positions, p, invf, qw, qw[:, pidx], kw, kw[:, pidx])

        r64 = lambda a: a[layer].reshape(1, DA_DH).astype(F32)
        oda = _attn(qn, kn, p, r64(lambda_q1), r64(lambda_k1), r64(lambda_q2), r64(lambda_k2),
                    subln_w[layer].reshape(1, DA_DV), B, S, lambda_init)

        x1 = _outproj(xf, mod3, hml, oda, wo_bf, S)
        xf = _ffn(x1, norm2_w[layer].reshape(1, D), mod3, wgu_bf, wdn_bf, S)
    return xf.reshape(B, S, D)
```

```python
import functools
import math

import jax
import jax.numpy as jnp
import numpy as np
from jax import lax
from jax.experimental import pallas as pl
from jax.experimental.pallas import tpu as pltpu

F32 = jnp.float32
BF16 = jnp.bfloat16

D_MODEL = 2048
CHUNK = 64
ML_HEADS = 4
ML_DQK = 128
ML_DV = 256
ML_CONV = 4
GATE_CAP = 15.0
DA_HEADS = 8
DA_DH = 64
DA_DV = 128
ROPE_DIM = 16
ROPE_THETA = 500000.0
ML_WIDTH = ML_HEADS * ML_DV
DA_WIDTH = DA_HEADS * DA_DV
QK_W = ML_HEADS * ML_DQK
D_FF = 5632
EPS = 1e-6
NEG = -1e30
LOG2E = 1.4426950408889634

LANES = 128
VMEM_LIMIT = 52 * 1024 * 1024
VMEM_LIMIT_BIG = 60 * 1024 * 1024

SEG_W = 2 * QK_W
SEG_QK, SEG_V, SEG_O, SEG_DQ, SEG_DK, SEG_DV = range(6)
P_COLS = 6 * SEG_W
GATE_LO = 3 * SEG_W
GATE_HI = GATE_LO + 2 * ML_HEADS
CHUNK_SHIFT = CHUNK.bit_length() - 1
DH_SHIFT = DA_DH.bit_length() - 1

ML_L = 256
AT_T = 512


def _cparams(sem, vmem=VMEM_LIMIT):
    return pltpu.CompilerParams(dimension_semantics=sem, vmem_limit_bytes=vmem)


def _dot(a, b):
    return jnp.dot(a, b, preferred_element_type=F32)


def _dot_nt(a, b):
    return lax.dot_general(a, b, (((1,), (1,)), ((), ())), preferred_element_type=F32)


def _split3(x):
    hi = x.astype(BF16)
    r1 = x - hi.astype(F32)
    mid = r1.astype(BF16)
    lo = (r1 - mid.astype(F32)).astype(BF16)
    return hi, mid, lo


def _ada_kernel(c_ref, w_ref, b_ref, o_ref):
    c = c_ref[...]
    a = (c * jax.nn.sigmoid(c)).astype(BF16)
    o_ref[...] = _dot(a, w_ref[...].astype(BF16)) + b_ref[...]


def _ada(c8, w, b):
    n = w.shape[1]
    tn = 1024
    return pl.pallas_call(
        _ada_kernel,
        out_shape=jax.ShapeDtypeStruct((8, n), F32),
        grid=(n // tn,),
        in_specs=[pl.BlockSpec((8, D_MODEL), lambda j: (0, 0)),
                  pl.BlockSpec((D_MODEL, tn), lambda j: (0, j)),
                  pl.BlockSpec((1, tn), lambda j: (0, j))],
        out_specs=pl.BlockSpec((8, tn), lambda j: (0, j)),
        compiler_params=_cparams(("parallel",)),
        name="adaln",
    )(c8, w, b)


def _norm_mod(x, gain, sh):
    ms = jnp.mean(x * x, axis=-1, keepdims=True)
    return (x * lax.rsqrt(ms + EPS)) * gain + sh


def _norm_mod_rows(x_ref, nw_ref, sc_ref, sh_ref, emit, tm, sub=128):
    gain = nw_ref[...] * (1.0 + sc_ref[...])
    sh = sh_ref[...]

    def body(r, carry):
        off = pl.multiple_of(r * sub, sub)
        emit(off, _norm_mod(x_ref[pl.ds(off, sub), :], gain, sh))
        return carry

    lax.fori_loop(0, tm // sub, body, 0)


IN_TM = 1024
IN_TN = 1024
IN_SUB = 256
IN_NSUB = IN_TM // IN_SUB


def _inproj_kernel(x_ref, nw_ref, sc_ref, sh_ref, w_ref, wg_ref, gu_ref, dn_ref, wo_ref, o_ref, g_ref, gub_ref, dnb_ref,
                   wob_ref, ha_ref, hb_ref, *, ni):
    i = pl.program_id(0)
    j = pl.program_id(1)
    hbufs = (ha_ref, hb_ref)

    def cast_slab():
        gub_ref[...] = gu_ref[...].astype(BF16)
        dnb_ref[...] = dn_ref[...].astype(BF16)
        wob_ref[...] = wo_ref[...].astype(BF16)

    def norm_slice(dst):
        gain = nw_ref[...] * (1.0 + sc_ref[...])
        hh = _norm_mod(x_ref[...], gain, sh_ref[...]).astype(BF16)
        rows = pl.ds(pl.multiple_of(j * IN_SUB, IN_SUB), IN_SUB)
        dst[rows, :] = hh
        g_ref[rows, :] = _dot_nt(hh, wg_ref[...])

    def matmul(src):
        o_ref[...] = _dot_nt(src[...], w_ref[...]).astype(BF16)

    has_norm = (i < ni) & (j < IN_NSUB)
    has_mm = i > 0

    @pl.when(jnp.logical_not(has_mm) & has_norm)
    def _():
        cast_slab()
        norm_slice(ha_ref)

    @pl.when(jnp.logical_not(has_mm) & jnp.logical_not(has_norm))
    def _():
        cast_slab()

    for par in range(2):
        mine = (i % 2) == par

        @pl.when(mine & has_mm & has_norm)
        def _():
            cast_slab()
            norm_slice(hbufs[par])
            matmul(hbufs[1 - par])

        @pl.when(mine & has_mm & jnp.logical_not(has_norm))
        def _():
            cast_slab()
            matmul(hbufs[1 - par])


def _inproj(x2, nw, mod3, w_bf, wg, w_gu, w_dn, w_o, tpb):
    t = x2.shape[0]
    tm, tn = IN_TM, IN_TN
    tiles_per_batch = tpb // tm
    ni = t // tm
    nj = P_COLS // tn
    last = ni - 1
    nslab = w_gu.shape[1] // LANES
    dn_rows = w_dn.shape[0] // nslab
    wo_rows = 32
    nslab_o = w_o.shape[0] // wo_rows
    assert max(nslab, nslab_o) <= ni * nj and dn_rows * nslab == w_dn.shape[0] and dn_rows % 16 == 0
    slab = lambda i, j: jnp.minimum(i * nj + j, nslab - 1)
    slab_o = lambda i, j: jnp.minimum(i * nj + j, nslab_o - 1)
    kern = functools.partial(_inproj_kernel, ni=ni)
    mrow = lambda k: pl.BlockSpec(
        (None, 1, D_MODEL), lambda i, j: ((jnp.minimum(i, last) // tiles_per_batch) * 6 + k, 0, 0))
    return pl.pallas_call(
        kern,
        out_shape=(jax.ShapeDtypeStruct((t, P_COLS), BF16), jax.ShapeDtypeStruct((t, LANES), F32),
                   jax.ShapeDtypeStruct(w_gu.shape, BF16), jax.ShapeDtypeStruct(w_dn.shape, BF16),
                   jax.ShapeDtypeStruct(w_o.shape, BF16)),
        grid=(ni + 1, nj),
        in_specs=[pl.BlockSpec((IN_SUB, D_MODEL),
                               lambda i, j: (jnp.minimum(i, last) * IN_NSUB + jnp.minimum(j, IN_NSUB - 1), 0)),
                  pl.BlockSpec((1, D_MODEL), lambda i, j: (0, 0)),
                  mrow(1), mrow(0),
                  pl.BlockSpec((tn, D_MODEL), lambda i, j: (j, 0)),
                  pl.BlockSpec((LANES, D_MODEL), lambda i, j: (0, 0)),
                  pl.BlockSpec((D_MODEL, LANES), lambda i, j: (0, slab(i, j))),
                  pl.BlockSpec((dn_rows, D_MODEL), lambda i, j: (slab(i, j), 0)),
                  pl.BlockSpec((wo_rows, D_MODEL), lambda i, j: (slab_o(i, j), 0))],
        out_specs=(pl.BlockSpec((tm, tn), lambda i, j: (jnp.maximum(i - 1, 0), jnp.where(i == 0, 0, j))),
                   pl.BlockSpec((tm, LANES), lambda i, j: (jnp.minimum(i, last), 0)),
                   pl.BlockSpec((D_MODEL, LANES), lambda i, j: (0, slab(i, j))),
                   pl.BlockSpec((dn_rows, D_MODEL), lambda i, j: (slab(i, j), 0)),
                   pl.BlockSpec((wo_rows, D_MODEL), lambda i, j: (slab_o(i, j), 0))),
        scratch_shapes=[pltpu.VMEM((tm, D_MODEL), BF16), pltpu.VMEM((tm, D_MODEL), BF16)],
        compiler_params=_cparams(("arbitrary", "arbitrary")),
        name="inproj",
    )(x2, nw, mod3, mod3, w_bf, wg, w_gu, w_dn, w_o)


def _wprep_kernel(w_hbm, o_ref, buf_ref, sem, *, tn, nblk, gate_lo, gate_rows):
    j = pl.program_id(0)

    def window(blk, slot):
        src = pl.multiple_of(blk * tn + jnp.where(blk * tn >= gate_lo, gate_rows, 0), 8)
        return pltpu.make_async_copy(w_hbm.at[pl.ds(src, tn), :], buf_ref.at[slot], sem.at[slot])

    @pl.when(j == 0)
    def _():
        window(0, 0).start()

    @pl.when(j + 1 < nblk)
    def _():
        window(j + 1, (j + 1) % 2).start()

    window(j, j % 2).wait()
    o_ref[...] = buf_ref[j % 2].astype(BF16)


def _wprep(w_t):
    tn = 512
    nblk = P_COLS // tn
    kern = functools.partial(_wprep_kernel, tn=tn, nblk=nblk, gate_lo=GATE_LO, gate_rows=GATE_HI - GATE_LO)
    return pl.pallas_call(
        kern,
        out_shape=jax.ShapeDtypeStruct((P_COLS, D_MODEL), BF16),
        grid=(nblk,),
        in_specs=[pl.BlockSpec(memory_space=pl.ANY)],
        out_specs=pl.BlockSpec((tn, D_MODEL), lambda j: (j, 0)),
        scratch_shapes=[pltpu.VMEM((2, tn, D_MODEL), F32), pltpu.SemaphoreType.DMA((2,))],
        compiler_params=_cparams(("arbitrary",)),
        name="wprep",
    )(w_t)


def _conv_silu(x, tail8, w4, b, row8):
    acc = b + w4[3:4, :] * x
    for k in (1, 2, 3):
        xr = pltpu.roll(x, k, 0)
        tr = pltpu.roll(tail8, k, 0)
        first = jnp.where(row8 < k, tr, xr[0:8, :])
        xs = jnp.concatenate([first, xr[8:, :]], axis=0)
        acc = acc + w4[3 - k:4 - k, :] * xs
    return acc * jax.nn.sigmoid(acc)


def _mlstm_kernel(qk_ref, v_ref, og_ref, g_ref, cw_ref, cb_ref, gb_ref, nw_ref, o_ref, ct_ref, tail_ref):
    L = ML_L
    c = pl.program_id(1)

    @pl.when(c == 0)
    def _():
        ct_ref[...] = jnp.zeros_like(ct_ref)
        tail_ref[...] = jnp.zeros_like(tail_ref)

    g = g_ref[...] + gb_ref[...]
    gc = GATE_CAP * jnp.tanh(g * (1.0 / GATE_CAP))
    lf_all = -jnp.log1p(jnp.exp(-gc))
    ri = lax.broadcasted_iota(jnp.int32, (L, L), 0)
    ci = lax.broadcasted_iota(jnp.int32, (L, L), 1)
    tri = ci <= ri
    trib = jnp.where(tri, 1.0, 0.0).astype(BF16)
    f_hi, f_mid, f_lo = _split3(lf_all)
    b_all = _dot(trib, f_hi) + _dot(trib, f_mid) + _dot(trib, f_lo)
    u_all = gc - pltpu.roll(b_all, LANES - ML_HEADS, 1)
    u_t = u_all.T

    row8 = lax.broadcasted_iota(jnp.int32, (8, LANES), 0)
    ones_b = jnp.ones((L, LANES), BF16)
    tails = tail_ref[...]

    for h in range(ML_HEADS):
        qs = slice(h * ML_DQK, (h + 1) * ML_DQK)
        ks = slice(QK_W + h * ML_DQK, QK_W + (h + 1) * ML_DQK)
        vs = slice(h * ML_DV, (h + 1) * ML_DV)
        q = _conv_silu(qk_ref[:, qs].astype(F32), tails[:, qs], cw_ref[:, qs], cb_ref[:, qs], row8)
        k = _conv_silu(qk_ref[:, ks].astype(F32), tails[:, ks], cw_ref[:, ks], cb_ref[:, ks], row8)
        q = q * (ML_DQK ** -0.5)

        b_col = jnp.broadcast_to(b_all[:, ML_HEADS + h:ML_HEADS + h + 1], (L, LANES))
        u_col = jnp.broadcast_to(u_all[:, h:h + 1], (L, LANES))
        b_end = b_all[L - 1:L, ML_HEADS + h:ML_HEADS + h + 1]
        u_row = u_t[h:h + 1, :]

        logd = jnp.concatenate([b_col, b_col], axis=1) + u_row
        d = jnp.exp(jnp.where(tri, logd, NEG))
        s = lax.dot_general(q.astype(BF16), k.astype(BF16), (((1,), (1,)), ((), ())),
                            preferred_element_type=F32) * d

        vaug = jnp.concatenate([v_ref[:, vs], ones_b], axis=1)
        ct = ct_ref[h]
        tot = _dot((q * jnp.exp(b_col)).astype(BF16), ct.astype(BF16)) + _dot(s.astype(BF16), vaug)
        den = tot[:, ML_DV:]
        den2 = jnp.concatenate([den, den], axis=1)
        hc = tot[:, :ML_DV] / jnp.maximum(jnp.abs(den2), 1.0)

        kw = (k * jnp.exp(b_end + u_col)).astype(BF16)
        upd = lax.dot_general(kw, vaug, (((0,), (0,)), ((), ())), preferred_element_type=F32)
        ct_ref[h] = jnp.exp(b_end) * ct + upd

        ms = jnp.mean(hc * hc, axis=-1, keepdims=True)
        hn = hc * lax.rsqrt(ms + EPS) * nw_ref[:, vs]
        o_ref[:, vs] = (hn * jax.nn.sigmoid(og_ref[:, vs].astype(F32))).astype(BF16)

    tail_ref[...] = qk_ref[L - 16:L, :].astype(F32)[8:16, :]


def _mlstm(p, g, cw, cb, gb, nw, batch, seq):
    t = p.shape[0]
    L = ML_L
    nc = seq // L
    return pl.pallas_call(
        _mlstm_kernel,
        out_shape=jax.ShapeDtypeStruct((t, ML_WIDTH), BF16),
        grid=(batch, nc),
        in_specs=[pl.BlockSpec((L, SEG_W), lambda b, c: (b * nc + c, SEG_QK)),
                  pl.BlockSpec((L, SEG_W), lambda b, c: (b * nc + c, SEG_V)),
                  pl.BlockSpec((L, SEG_W), lambda b, c: (b * nc + c, SEG_O)),
                  pl.BlockSpec((L, LANES), lambda b, c: (b * nc + c, 0)),
                  pl.BlockSpec((ML_CONV, SEG_W), lambda b, c: (0, 0)),
                  pl.BlockSpec((1, SEG_W), lambda b, c: (0, 0)),
                  pl.BlockSpec((1, LANES), lambda b, c: (0, 0)),
                  pl.BlockSpec((1, ML_WIDTH), lambda b, c: (0, 0))],
        out_specs=pl.BlockSpec((L, ML_WIDTH), lambda b, c: (b * nc + c, 0)),
        scratch_shapes=[pltpu.VMEM((ML_HEADS, ML_DQK, ML_DV + LANES), F32),
                        pltpu.VMEM((8, SEG_W), F32)],
        compiler_params=_cparams(("parallel", "arbitrary")),
        name="mlstm",
    )(p, p, p, g, cw, cb, gb, nw)


def _qkprep_kernel(pos_ref, q_ref, k_ref, invf_ref, qw_ref, qpw_ref, kw_ref, kpw_ref, qo_ref, ko_ref):
    half = ROPE_DIM // 2
    tm = q_ref.shape[0]
    pos = pos_ref[0].astype(F32)
    ang = jnp.concatenate([invf_ref[...]] * (tm // LANES), axis=1) * pos
    fi = lax.broadcasted_iota(jnp.int32, (4 * half, LANES), 0)
    li = lax.broadcasted_iota(jnp.int32, (4 * half, LANES), 1)
    dl = li & (DA_DH - 1)
    hit = (dl < ROPE_DIM) & ((dl & (half - 1)) == (fi & (half - 1))) & (fi < 3 * half)
    e_cos = jnp.where(hit, 1.0, 0.0).astype(BF16)
    e_sin = jnp.where(hit, jnp.where(dl < half, -1.0, 1.0), 0.0).astype(BF16)

    def expand(t8, e):
        hi, mid, lo = _split3(t8)
        parts = jnp.concatenate([hi.astype(F32), mid.astype(F32), lo.astype(F32), jnp.zeros_like(t8)], axis=0)
        return lax.dot_general(parts.astype(BF16), e, (((0,), (0,)), ((), ())), preferred_element_type=F32)

    lane = lax.broadcasted_iota(jnp.int32, (1, LANES), 1)
    cs = expand(jnp.cos(ang), e_cos) + jnp.where((lane & (DA_DH - 1)) < ROPE_DIM, 0.0, 1.0)
    sn = expand(jnp.sin(ang), e_sin)
    ri = lax.broadcasted_iota(jnp.int32, (LANES, LANES), 0)
    ci = lax.broadcasted_iota(jnp.int32, (LANES, LANES), 1)
    dc = ci & (DA_DH - 1)
    partner = jnp.where(dc < half, ci + half, jnp.where(dc < ROPE_DIM, ci - half, -1))
    perm = jnp.where(ri == partner, 1.0, 0.0).astype(BF16)
    mean_blk = jnp.where((ri >> DH_SHIFT) == (ci >> DH_SHIFT), 1.0 / DA_DH, 0.0).astype(BF16)

    def prep(x_ref, w_ref, pw_ref, o_ref, scale):
        a = (w_ref[...] * scale) * cs
        bc = (pw_ref[...] * scale) * sn
        for hh in range(DA_HEADS):
            cols = slice(hh * LANES, (hh + 1) * LANES)
            xb = x_ref[:, cols]
            xf = xb.astype(F32)
            r = lax.rsqrt(_dot((xf * xf).astype(BF16), mean_blk) + EPS)
            px = _dot(xb, perm)
            o_ref[:, cols] = (r * (xf * a + px * bc)).astype(BF16)

    prep(q_ref, qw_ref, qpw_ref, qo_ref, DA_DH ** -0.5 * LOG2E)
    prep(k_ref, kw_ref, kpw_ref, ko_ref, 1.0)


def _qkprep(positions, p, invf, qw, qpw, kw, kpw):
    t = p.shape[0]
    tm = 512
    pos3 = positions.reshape(t // tm, 1, tm)
    vec = pl.BlockSpec((1, LANES), lambda i: (0, 0))
    return pl.pallas_call(
        _qkprep_kernel,
        out_shape=(jax.ShapeDtypeStruct((t, SEG_W), BF16), jax.ShapeDtypeStruct((t, SEG_W), BF16)),
        grid=(t // tm,),
        in_specs=[pl.BlockSpec((1, 1, tm), lambda i: (i, 0, 0)),
                  pl.BlockSpec((tm, SEG_W), lambda i: (i, SEG_DQ)),
                  pl.BlockSpec((tm, SEG_W), lambda i: (i, SEG_DK)),
                  pl.BlockSpec((ROPE_DIM // 2, LANES), lambda i: (0, 0)),
                  vec, vec, vec, vec],
        out_specs=(pl.BlockSpec((tm, SEG_W), lambda i: (i, 0)),
                   pl.BlockSpec((tm, SEG_W), lambda i: (i, 0))),
        compiler_params=_cparams(("parallel",)),
        name="qkprep",
    )(pos3, p, p, invf, qw, qpw, kw, kpw)


AT_GROUP = 6


def _attn_kernel(q_ref, k_ref, v_ref, lq1_ref, lk1_ref, lq2_ref, lk2_ref, sw_ref, o_ref, m_ref, acc_ref,
                 sa_ref, sb_ref, *, lambda_init, nq):
    T = AT_T
    lane = lax.broadcasted_iota(jnp.int32, (T, LANES), 1)
    ones_b = jnp.ones((T, LANES), BF16)
    sbufs = (sa_ref, sb_ref)
    stages = [(qi, kt) for qi in range(nq) for kt in [qi] + list(range(qi))]

    H = T // 2

    def chunk_mask(rows, cols, row0):
        r = lax.broadcasted_iota(jnp.int32, (rows, cols), 0) + row0
        cc = lax.broadcasted_iota(jnp.int32, (rows, cols), 1)
        return (cc >> CHUNK_SHIFT) <= (r >> CHUNK_SHIFT)

    def score(t):
        qi, kt = stages[t]
        q = q_ref[qi * T:(qi + 1) * T, :]
        kb = k_ref[kt * T:(kt + 1) * T, :]
        zero = jnp.zeros_like(q)
        for mi, qm in enumerate((jnp.where(lane < DA_DH, q, zero), jnp.where(lane >= DA_DH, q, zero))):
            if kt == qi:
                sbufs[t % 2][mi, :H, :H] = jnp.where(chunk_mask(H, H, 0), _dot_nt(qm[:H], kb[:H]), NEG)
                sbufs[t % 2][mi, H:, :] = jnp.where(chunk_mask(H, T, H), _dot_nt(qm[H:], kb), NEG)
            else:
                sbufs[t % 2][mi] = _dot_nt(qm, kb)

    def softmax_pv(s, m_old, acc_old, vaug):
        rmax = jnp.max(s, axis=1, keepdims=True)
        ntile = s.shape[1] // LANES
        if m_old is None:
            m_new = jnp.broadcast_to(rmax, (s.shape[0], LANES))
            p = jnp.exp2(s - jnp.concatenate([m_new] * ntile, axis=1)).astype(BF16)
            return m_new, _dot(p, vaug)
        m_new = jnp.maximum(m_old, rmax)
        al = jnp.exp2(m_old - m_new)
        p = jnp.exp2(s - jnp.concatenate([m_new] * ntile, axis=1)).astype(BF16)
        return m_new, acc_old * jnp.concatenate([al, al], axis=1) + _dot(p, vaug)

    def consume(t):
        qi, kt = stages[t]
        vaug = jnp.concatenate([v_ref[kt * T:(kt + 1) * T, :], ones_b], axis=1)
        for mi in range(2):
            if kt == qi:
                m_ref[mi, :H], acc_ref[mi, :H] = softmax_pv(sbufs[t % 2][mi, :H, :H], None, None, vaug[:H])
                m_ref[mi, H:], acc_ref[mi, H:] = softmax_pv(sbufs[t % 2][mi, H:, :], None, None, vaug)
            else:
                m_ref[mi], acc_ref[mi] = softmax_pv(sbufs[t % 2][mi], m_ref[mi], acc_ref[mi], vaug)
        if t + 1 == len(stages) or stages[t + 1][0] != qi:
            finalize(qi)

    def finalize(qi):
        acc0 = acc_ref[0]
        acc1 = acc_ref[1]
        o0 = acc0[:, :DA_DV] / acc0[:, DA_DV:]
        o1 = acc1[:, :DA_DV] / acc1[:, DA_DV:]
        lam = (jnp.exp(jnp.sum(lq1_ref[...] * lk1_ref[...], axis=-1, keepdims=True))
               - jnp.exp(jnp.sum(lq2_ref[...] * lk2_ref[...], axis=-1, keepdims=True)) + lambda_init)
        o = o0 - lam * o1
        ms = jnp.mean(o * o, axis=-1, keepdims=True)
        o_ref[qi * T:(qi + 1) * T, :] = (o * lax.rsqrt(ms + EPS) * (sw_ref[...] * (1.0 - lambda_init))).astype(BF16)

    always = (pl.program_id(0) >= 0, pl.program_id(1) >= 0)
    score(0)
    for g0 in range(0, len(stages), AT_GROUP):
        @pl.when(always[(g0 // AT_GROUP) % 2])
        def _():
            for t in range(g0, min(g0 + AT_GROUP, len(stages))):
                if t + 1 < len(stages):
                    score(t + 1)
                consume(t)


def _attn(qn, kn, p, lq1, lk1, lq2, lk2, sw, batch, seq, lambda_init):
    t = qn.shape[0]
    T = AT_T
    vcol0 = SEG_DV * SEG_W // LANES
    vec64 = pl.BlockSpec((1, DA_DH), lambda b, h: (0, 0))
    kern = functools.partial(_attn_kernel, lambda_init=lambda_init, nq=seq // T)
    return pl.pallas_call(
        kern,
        out_shape=jax.ShapeDtypeStruct((t, DA_WIDTH), BF16),
        grid=(batch, DA_HEADS),
        in_specs=[pl.BlockSpec((seq, LANES), lambda b, h: (b, h)),
                  pl.BlockSpec((seq, LANES), lambda b, h: (b, h)),
                  pl.BlockSpec((seq, LANES), lambda b, h: (b, vcol0 + h)),
                  vec64, vec64, vec64, vec64,
                  pl.BlockSpec((1, DA_DV), lambda b, h: (0, 0))],
        out_specs=pl.BlockSpec((seq, LANES), lambda b, h: (b, h)),
        scratch_shapes=[pltpu.VMEM((2, T, LANES), F32), pltpu.VMEM((2, T, 2 * LANES), F32),
                        pltpu.VMEM((2, T, T), F32), pltpu.VMEM((2, T, T), F32)],
        compiler_params=_cparams(("parallel", "parallel")),
        name="diffattn",
    )(qn, kn, p, lq1, lk1, lq2, lk2, sw)


def _outproj_kernel(x_ref, g_ref, a_ref, b_ref, wa_ref, wb_ref, o_ref):
    mix = _dot(a_ref[...], wa_ref[...]) + _dot(b_ref[...], wb_ref[...])
    o_ref[...] = x_ref[...] + g_ref[...] * mix


def _outproj(x2, mod3, hml, oda, w_bf, tpb):
    t = x2.shape[0]
    tm = 512
    tiles_per_batch = tpb // tm
    return pl.pallas_call(
        _outproj_kernel,
        out_shape=jax.ShapeDtypeStruct((t, D_MODEL), F32),
        grid=(t // tm,),
        in_specs=[pl.BlockSpec((tm, D_MODEL), lambda i: (i, 0)),
                  pl.BlockSpec((None, 1, D_MODEL), lambda i: ((i // tiles_per_batch) * 6 + 2, 0, 0)),
                  pl.BlockSpec((tm, ML_WIDTH), lambda i: (i, 0)),
                  pl.BlockSpec((tm, DA_WIDTH), lambda i: (i, 0)),
                  pl.BlockSpec((ML_WIDTH, D_MODEL), lambda i: (0, 0)),
                  pl.BlockSpec((DA_WIDTH, D_MODEL), lambda i: (1, 0))],
        out_specs=pl.BlockSpec((tm, D_MODEL), lambda i: (i, 0)),
        compiler_params=_cparams(("parallel",)),
        name="outproj",
    )(x2, mod3, hml, oda, w_bf, w_bf)


def _ffn_kernel(x_ref, nw_ref, sc_ref, sh_ref, gt_ref, wg_ref, wu_ref, wd_ref, o_ref, h_ref, *, tm, sub, nj):
    j = pl.program_id(1)

    @pl.when(j == 0)
    def _():
        def emit(off, h):
            h_ref[pl.ds(off, sub), :] = h.astype(BF16)
        _norm_mod_rows(x_ref, nw_ref, sc_ref, sh_ref, emit, tm, sub)

    def down():
        h = h_ref[...]
        g = _dot(h, wg_ref[...])
        u = _dot(h, wu_ref[...])
        a = (g * jax.nn.sigmoid(g) * u).astype(BF16)
        return _dot(a, wd_ref[...])

    @pl.when(j == 0)
    def _():
        o_ref[...] = down()

    @pl.when((j > 0) & (j < nj - 1))
    def _():
        o_ref[...] += down()

    @pl.when(j == nj - 1)
    def _():
        o_ref[...] = x_ref[...] + gt_ref[...] * (o_ref[...] + down())


def _ffn(x1, nw, mod3, wgu_bf, wd_bf, tpb):
    t = x1.shape[0]
    tm, tf, sub = 1024, 512, 128
    tiles_per_batch = tpb // tm
    nj = D_FF // tf
    kern = functools.partial(_ffn_kernel, tm=tm, sub=sub, nj=nj)
    mrow = lambda k: pl.BlockSpec((None, 1, D_MODEL), lambda i, j: ((i // tiles_per_batch) * 6 + k, 0, 0))
    return pl.pallas_call(
        kern,
        out_shape=jax.ShapeDtypeStruct((t, D_MODEL), F32),
        grid=(t // tm, nj),
        in_specs=[pl.BlockSpec((tm, D_MODEL), lambda i, j: (i, 0)),
                  pl.BlockSpec((1, D_MODEL), lambda i, j: (0, 0)),
                  mrow(4), mrow(3), mrow(5),
                  pl.BlockSpec((D_MODEL, tf), lambda i, j: (0, j)),
                  pl.BlockSpec((D_MODEL, tf), lambda i, j: (0, nj + j)),
                  pl.BlockSpec((tf, D_MODEL), lambda i, j: (j, 0))],
        out_specs=pl.BlockSpec((tm, D_MODEL), lambda i, j: (i, 0)),
        scratch_shapes=[pltpu.VMEM((tm, D_MODEL), BF16)],
        compiler_params=_cparams(("parallel", "arbitrary"), VMEM_LIMIT_BIG),
        name="ffn",
    )(x1, nw, mod3, mod3, mod3, wgu_bf, wgu_bf, wd_bf)


def _rope_partner_index():
    idx = np.arange(LANES)
    d = idx % DA_DH
    half = ROPE_DIM // 2
    return np.where(d < half, idx + half, np.where(d < ROPE_DIM, idx - half, idx))


def kernel(x, c, positions, norm1_w, norm2_w, w_ada, b_ada, w_in, mlstm_conv_w, mlstm_conv_b, mlstm_gate_b,
           mlstm_norm_w, q_norm_w, k_norm_w, lambda_q1, lambda_k1, lambda_q2, lambda_k2, subln_w, w_out,
           w_gate_up, w_down):
    B, S, D = x.shape
    T = B * S
    depth = w_in.shape[0]
    xf = x.reshape(T, D)
    c8 = jnp.pad(c, ((0, 8 - B), (0, 0)))

    inv_freq = ROPE_THETA ** (-jnp.arange(0, ROPE_DIM, 2, dtype=F32) / ROPE_DIM)
    invf = jnp.broadcast_to(inv_freq[:, None], (ROPE_DIM // 2, LANES))
    pidx = _rope_partner_index()

    for layer in range(depth):
        lambda_init = 0.8 - 0.6 * math.exp(-0.3 * layer)
        mod = _ada(c8, w_ada[layer], b_ada[layer].reshape(1, -1))
        mod3 = mod[:B].reshape(B * 6, 1, D)

        w_l = w_in[layer]
        w_t = w_l.T
        w_bf = _wprep(w_t)
        wgate = jnp.pad(w_t[GATE_LO:GATE_HI], ((0, LANES - 2 * ML_HEADS), (0, 0)))
        wg = wgate.astype(BF16)

        p, g, wgu_bf, wdn_bf, wo_bf = _inproj(xf, norm1_w[layer].reshape(1, D), mod3, w_bf, wg,
                                              w_gate_up[layer], w_down[layer], w_out[layer], S)

        gb = jnp.pad(mlstm_gate_b[layer].reshape(1, -1), ((0, 0), (0, LANES - 2 * ML_HEADS)))
        hml = _mlstm(p, g, mlstm_conv_w[layer], mlstm_conv_b[layer].reshape(1, -1), gb,
                     mlstm_norm_w[layer].reshape(1, ML_WIDTH), B, S)

        qw = jnp.tile(q_norm_w[layer], 2).reshape(1, LANES)
        kw = jnp.tile(k_norm_w[layer], 2).reshape(1, LANES)
        qn, kn = _qkprep(positions, p, invf, qw, qw[:, pidx], kw, kw[:, pidx])

        r64 = lambda a: a[layer].reshape(1, DA_DH).astype(F32)
        oda = _attn(qn, kn, p, r64(lambda_q1), r64(lambda_k1), r64(lambda_q2), r64(lambda_k2),
                    subln_w[layer].reshape(1, DA_DV), B, S, lambda_init)

        x1 = _outproj(xf, mod3, hml, oda, wo_bf, S)
        xf = _ffn(x1, norm2_w[layer].reshape(1, D), mod3, wgu_bf, wdn_bf, S)
    return xf.reshape(B, S, D)
```

```python
import functools
import math

import jax
import jax.numpy as jnp
import numpy as np
from jax import lax
from jax.experimental import pallas as pl
from jax.experimental.pallas import tpu as pltpu

F32 = jnp.float32
BF16 = jnp.bfloat16

D_MODEL = 2048
CHUNK = 64
ML_HEADS = 4
ML_DQK = 128
ML_DV = 256
ML_CONV = 4
GATE_CAP = 15.0
DA_HEADS = 8
DA_DH = 64
DA_DV = 128
ROPE_DIM = 16
ROPE_THETA = 500000.0
ML_WIDTH = ML_HEADS * ML_DV
DA_WIDTH = DA_HEADS * DA_DV
QK_W = ML_HEADS * ML_DQK
D_FF = 5632
EPS = 1e-6
NEG = -1e30
LOG2E = 1.4426950408889634

LANES = 128
VMEM_LIMIT = 52 * 1024 * 1024
VMEM_LIMIT_BIG = 60 * 1024 * 1024

SEG_W = 2 * QK_W
SEG_QK, SEG_V, SEG_O, SEG_DQ, SEG_DK, SEG_DV = range(6)
P_COLS = 6 * SEG_W
GATE_LO = 3 * SEG_W
GATE_HI = GATE_LO + 2 * ML_HEADS
CHUNK_SHIFT = CHUNK.bit_length() - 1
DH_SHIFT = DA_DH.bit_length() - 1

ML_L = 256
AT_T = 512


def _cparams(sem, vmem=VMEM_LIMIT):
    return pltpu.CompilerParams(dimension_semantics=sem, vmem_limit_bytes=vmem)


def _dot(a, b):
    return jnp.dot(a, b, preferred_element_type=F32)


def _dot_nt(a, b):
    return lax.dot_general(a, b, (((1,), (1,)), ((), ())), preferred_element_type=F32)


def _split3(x):
    hi = x.astype(BF16)
    r1 = x - hi.astype(F32)
    mid = r1.astype(BF16)
    lo = (r1 - mid.astype(F32)).astype(BF16)
    return hi, mid, lo


def _ada_kernel(c_ref, w_ref, b_ref, o_ref):
    c = c_ref[...]
    a = (c * jax.nn.sigmoid(c)).astype(BF16)
    o_ref[...] = _dot(a, w_ref[...].astype(BF16)) + b_ref[...]


def _ada(c8, w, b):
    n = w.shape[1]
    tn = 1024
    return pl.pallas_call(
        _ada_kernel,
        out_shape=jax.ShapeDtypeStruct((8, n), F32),
        grid=(n // tn,),
        in_specs=[pl.BlockSpec((8, D_MODEL), lambda j: (0, 0)),
                  pl.BlockSpec((D_MODEL, tn), lambda j: (0, j)),
                  pl.BlockSpec((1, tn), lambda j: (0, j))],
        out_specs=pl.BlockSpec((8, tn), lambda j: (0, j)),
        compiler_params=_cparams(("parallel",)),
        name="adaln",
    )(c8, w, b)


def _norm_mod(x, gain, sh):
    ms = jnp.mean(x * x, axis=-1, keepdims=True)
    return (x * lax.rsqrt(ms + EPS)) * gain + sh


def _norm_mod_rows(x_ref, nw_ref, sc_ref, sh_ref, emit, tm, sub=128):
    gain = nw_ref[...] * (1.0 + sc_ref[...])
    sh = sh_ref[...]

    def body(r, carry):
        off = pl.multiple_of(r * sub, sub)
        emit(off, _norm_mod(x_ref[pl.ds(off, sub), :], gain, sh))
        return carry

    lax.fori_loop(0, tm // sub, body, 0)


IN_TM = 1024
IN_TN = 1024
IN_SUB = 256
IN_NSUB = IN_TM // IN_SUB


def _inproj_kernel(x_ref, nw_ref, sc_ref, sh_ref, w_ref, wg_ref, gu_ref, dn_ref, wo_ref, o_ref, g_ref, gub_ref, dnb_ref,
                   wob_ref, ha_ref, hb_ref, *, ni):
    i = pl.program_id(0)
    j = pl.program_id(1)
    hbufs = (ha_ref, hb_ref)

    def cast_slab():
        gub_ref[...] = gu_ref[...].astype(BF16)
        dnb_ref[...] = dn_ref[...].astype(BF16)
        wob_ref[...] = wo_ref[...].astype(BF16)

    def norm_slice(dst):
        gain = nw_ref[...] * (1.0 + sc_ref[...])
        hh = _norm_mod(x_ref[...], gain, sh_ref[...]).astype(BF16)
        rows = pl.ds(pl.multiple_of(j * IN_SUB, IN_SUB), IN_SUB)
        dst[rows, :] = hh
        g_ref[rows, :] = _dot_nt(hh, wg_ref[...])

    def matmul(src):
        o_ref[...] = _dot_nt(src[...], w_ref[...]).astype(BF16)

    has_norm = (i < ni) & (j < IN_NSUB)
    has_mm = i > 0

    @pl.when(jnp.logical_not(has_mm) & has_norm)
    def _():
        cast_slab()
        norm_slice(ha_ref)

    @pl.when(jnp.logical_not(has_mm) & jnp.logical_not(has_norm))
    def _():
        cast_slab()

    for par in range(2):
        mine = (i % 2) == par

        @pl.when(mine & has_mm & has_norm)
        def _():
            cast_slab()
            norm_slice(hbufs[par])
            matmul(hbufs[1 - par])

        @pl.when(mine & has_mm & jnp.logical_not(has_norm))
        def _():
            cast_slab()
            matmul(hbufs[1 - par])


def _inproj(x2, nw, mod3, w_bf, wg, w_gu, w_dn, w_o, tpb):
    t = x2.shape[0]
    tm, tn = IN_TM, IN_TN
    tiles_per_batch = tpb // tm
    ni = t // tm
    nj = P_COLS // tn
    last = ni - 1
    nslab = w_gu.shape[1] // LANES
    dn_rows = w_dn.shape[0] // nslab
    wo_rows = 32
    nslab_o = w_o.shape[0] // wo_rows
    assert max(nslab, nslab_o) <= ni * nj and dn_rows * nslab == w_dn.shape[0] and dn_rows % 16 == 0
    slab = lambda i, j: jnp.minimum(i * nj + j, nslab - 1)
    slab_o = lambda i, j: jnp.minimum(i * nj + j, nslab_o - 1)
    kern = functools.partial(_inproj_kernel, ni=ni)
    mrow = lambda k: pl.BlockSpec(
        (None, 1, D_MODEL), lambda i, j: ((jnp.minimum(i, last) // tiles_per_batch) * 6 + k, 0, 0))
    return pl.pallas_call(
        kern,
        out_shape=(jax.ShapeDtypeStruct((t, P_COLS), BF16), jax.ShapeDtypeStruct((t, LANES), F32),
                   jax.ShapeDtypeStruct(w_gu.shape, BF16), jax.ShapeDtypeStruct(w_dn.shape, BF16),
                   jax.ShapeDtypeStruct(w_o.shape, BF16)),
        grid=(ni + 1, nj),
        in_specs=[pl.BlockSpec((IN_SUB, D_MODEL),
                               lambda i, j: (jnp.minimum(i, last) * IN_NSUB + jnp.minimum(j, IN_NSUB - 1), 0)),
                  pl.BlockSpec((1, D_MODEL), lambda i, j: (0, 0)),
                  mrow(1), mrow(0),
                  pl.BlockSpec((tn, D_MODEL), lambda i, j: (j, 0)),
                  pl.BlockSpec((LANES, D_MODEL), lambda i, j: (0, 0)),
                  pl.BlockSpec((D_MODEL, LANES), lambda i, j: (0, slab(i, j))),
                  pl.BlockSpec((dn_rows, D_MODEL), lambda i, j: (slab(i, j), 0)),
                  pl.BlockSpec((wo_rows, D_MODEL), lambda i, j: (slab_o(i, j), 0))],
        out_specs=(pl.BlockSpec((tm, tn), lambda i, j: (jnp.maximum(i - 1, 0), jnp.where(i == 0, 0, j))),
                   pl.BlockSpec((tm, LANES), lambda i, j: (jnp.minimum(i, last), 0)),
                   pl.BlockSpec((D_MODEL, LANES), lambda i, j: (0, slab(i, j))),
                   pl.BlockSpec((dn_rows, D_MODEL), lambda i, j: (slab(i, j), 0)),
                   pl.BlockSpec((wo_rows, D_MODEL), lambda i, j: (slab_o(i, j), 0))),
        scratch_shapes=[pltpu.VMEM((tm, D_MODEL), BF16), pltpu.VMEM((tm, D_MODEL), BF16)],
        compiler_params=_cparams(("arbitrary", "arbitrary")),
        name="inproj",
    )(x2, nw, mod3, mod3, w_bf, wg, w_gu, w_dn, w_o)


def _wprep_kernel(w_hbm, o_ref, buf_ref, sem, *, tn, nblk, gate_lo, gate_rows):
    j = pl.program_id(0)

    def window(blk, slot):
        src = pl.multiple_of(blk * tn + jnp.where(blk * tn >= gate_lo, gate_rows, 0), 8)
        return pltpu.make_async_copy(w_hbm.at[pl.ds(src, tn), :], buf_ref.at[slot], sem.at[slot])

    @pl.when(j == 0)
    def _():
        window(0, 0).start()

    @pl.when(j + 1 < nblk)
    def _():
        window(j + 1, (j + 1) % 2).start()

    window(j, j % 2).wait()
    o_ref[...] = buf_ref[j % 2].astype(BF16)


def _wprep(w_t):
    tn = 512
    nblk = P_COLS // tn
    kern = functools.partial(_wprep_kernel, tn=tn, nblk=nblk, gate_lo=GATE_LO, gate_rows=GATE_HI - GATE_LO)
    return pl.pallas_call(
        kern,
        out_shape=jax.ShapeDtypeStruct((P_COLS, D_MODEL), BF16),
        grid=(nblk,),
        in_specs=[pl.BlockSpec(memory_space=pl.ANY)],
        out_specs=pl.BlockSpec((tn, D_MODEL), lambda j: (j, 0)),
        scratch_shapes=[pltpu.VMEM((2, tn, D_MODEL), F32), pltpu.SemaphoreType.DMA((2,))],
        compiler_params=_cparams(("arbitrary",)),
        name="wprep",
    )(w_t)


def _conv_silu(x, tail8, w4, b, row8):
    acc = b + w4[3:4, :] * x
    for k in (1, 2, 3):
        xr = pltpu.roll(x, k, 0)
        tr = pltpu.roll(tail8, k, 0)
        first = jnp.where(row8 < k, tr, xr[0:8, :])
        xs = jnp.concatenate([first, xr[8:, :]], axis=0)
        acc = acc + w4[3 - k:4 - k, :] * xs
    return acc * jax.nn.sigmoid(acc)


def _mlstm_kernel(qk_ref, v_ref, og_ref, g_ref, cw_ref, cb_ref, gb_ref, nw_ref, o_ref, ct_ref, tail_ref):
    L = ML_L
    c = pl.program_id(1)

    @pl.when(c == 0)
    def _():
        ct_ref[...] = jnp.zeros_like(ct_ref)
        tail_ref[...] = jnp.zeros_like(tail_ref)

    g = g_ref[...] + gb_ref[...]
    gc = GATE_CAP * jnp.tanh(g * (1.0 / GATE_CAP))
    lf_all = -jnp.log1p(jnp.exp(-gc))
    ri = lax.broadcasted_iota(jnp.int32, (L, L), 0)
    ci = lax.broadcasted_iota(jnp.int32, (L, L), 1)
    tri = ci <= ri
    trib = jnp.where(tri, 1.0, 0.0).astype(BF16)
    f_hi, f_mid, f_lo = _split3(lf_all)
    b_all = _dot(trib, f_hi) + _dot(trib, f_mid) + _dot(trib, f_lo)
    u_all = gc - pltpu.roll(b_all, LANES - ML_HEADS, 1)
    u_t = u_all.T

    row8 = lax.broadcasted_iota(jnp.int32, (8, LANES), 0)
    ones_b = jnp.ones((L, LANES), BF16)
    tails = tail_ref[...]

    for h in range(ML_HEADS):
        qs = slice(h * ML_DQK, (h + 1) * ML_DQK)
        ks = slice(QK_W + h * ML_DQK, QK_W + (h + 1) * ML_DQK)
        vs = slice(h * ML_DV, (h + 1) * ML_DV)
        q = _conv_silu(qk_ref[:, qs].astype(F32), tails[:, qs], cw_ref[:, qs], cb_ref[:, qs], row8)
        k = _conv_silu(qk_ref[:, ks].astype(F32), tails[:, ks], cw_ref[:, ks], cb_ref[:, ks], row8)
        q = q * (ML_DQK ** -0.5)

        b_col = jnp.broadcast_to(b_all[:, ML_HEADS + h:ML_HEADS + h + 1], (L, LANES))
        u_col = jnp.broadcast_to(u_all[:, h:h + 1], (L, LANES))
        b_end = b_all[L - 1:L, ML_HEADS + h:ML_HEADS + h + 1]
        u_row = u_t[h:h + 1, :]

        logd = jnp.concatenate([b_col, b_col], axis=1) + u_row
        d = jnp.exp(jnp.where(tri, logd, NEG))
        s = lax.dot_general(q.astype(BF16), k.astype(BF16), (((1,), (1,)), ((), ())),
                            preferred_element_type=F32) * d

        vaug = jnp.concatenate([v_ref[:, vs], ones_b], axis=1)
        ct = ct_ref[h]
        tot = _dot((q * jnp.exp(b_col)).astype(BF16), ct.astype(BF16)) + _dot(s.astype(BF16), vaug)
        den = tot[:, ML_DV:]
        den2 = jnp.concatenate([den, den], axis=1)
        hc = tot[:, :ML_DV] / jnp.maximum(jnp.abs(den2), 1.0)

        kw = (k * jnp.exp(b_end + u_col)).astype(BF16)
        upd = lax.dot_general(kw, vaug, (((0,), (0,)), ((), ())), preferred_element_type=F32)
        ct_ref[h] = jnp.exp(b_end) * ct + upd

        ms = jnp.mean(hc * hc, axis=-1, keepdims=True)
        hn = hc * lax.rsqrt(ms + EPS) * nw_ref[:, vs]
        o_ref[:, vs] = (hn * jax.nn.sigmoid(og_ref[:, vs].astype(F32))).astype(BF16)

    tail_ref[...] = qk_ref[L - 16:L, :].astype(F32)[8:16, :]


def _mlstm(p, g, cw, cb, gb, nw, batch, seq):
    t = p.shape[0]
    L = ML_L
    nc = seq // L
    return pl.pallas_call(
        _mlstm_kernel,
        out_shape=jax.ShapeDtypeStruct((t, ML_WIDTH), BF16),
        grid=(batch, nc),
        in_specs=[pl.BlockSpec((L, SEG_W), lambda b, c: (b * nc + c, SEG_QK)),
                  pl.BlockSpec((L, SEG_W), lambda b, c: (b * nc + c, SEG_V)),
                  pl.BlockSpec((L, SEG_W), lambda b, c: (b * nc + c, SEG_O)),
                  pl.BlockSpec((L, LANES), lambda b, c: (b * nc + c, 0)),
                  pl.BlockSpec((ML_CONV, SEG_W), lambda b, c: (0, 0)),
                  pl.BlockSpec((1, SEG_W), lambda b, c: (0, 0)),
                  pl.BlockSpec((1, LANES), lambda b, c: (0, 0)),
                  pl.BlockSpec((1, ML_WIDTH), lambda b, c: (0, 0))],
        out_specs=pl.BlockSpec((L, ML_WIDTH), lambda b, c: (b * nc + c, 0)),
        scratch_shapes=[pltpu.VMEM((ML_HEADS, ML_DQK, ML_DV + LANES), F32),
                        pltpu.VMEM((8, SEG_W), F32)],
        compiler_params=_cparams(("parallel", "arbitrary")),
        name="mlstm",
    )(p, p, p, g, cw, cb, gb, nw)


def _qkprep_kernel(pos_ref, q_ref, k_ref, invf_ref, qw_ref, qpw_ref, kw_ref, kpw_ref, qo_ref, ko_ref):
    half = ROPE_DIM // 2
    tm = q_ref.shape[0]
    pos = pos_ref[0].astype(F32)
    ang = jnp.concatenate([invf_ref[...]] * (tm // LANES), axis=1) * pos
    fi = lax.broadcasted_iota(jnp.int32, (4 * half, LANES), 0)
    li = lax.broadcasted_iota(jnp.int32, (4 * half, LANES), 1)
    dl = li & (DA_DH - 1)
    hit = (dl < ROPE_DIM) & ((dl & (half - 1)) == (fi & (half - 1))) & (fi < 3 * half)
    e_cos = jnp.where(hit, 1.0, 0.0).astype(BF16)
    e_sin = jnp.where(hit, jnp.where(dl < half, -1.0, 1.0), 0.0).astype(BF16)

    def expand(t8, e):
        hi, mid, lo = _split3(t8)
        parts = jnp.concatenate([hi.astype(F32), mid.astype(F32), lo.astype(F32), jnp.zeros_like(t8)], axis=0)
        return lax.dot_general(parts.astype(BF16), e, (((0,), (0,)), ((), ())), preferred_element_type=F32)

    lane = lax.broadcasted_iota(jnp.int32, (1, LANES), 1)
    cs = expand(jnp.cos(ang), e_cos) + jnp.where((lane & (DA_DH - 1)) < ROPE_DIM, 0.0, 1.0)
    sn = expand(jnp.sin(ang), e_sin)
    ri = lax.broadcasted_iota(jnp.int32, (LANES, LANES), 0)
    ci = lax.broadcasted_iota(jnp.int32, (LANES, LANES), 1)
    dc = ci & (DA_DH - 1)
    partner = jnp.where(dc < half, ci + half, jnp.where(dc < ROPE_DIM, ci - half, -1))
    perm = jnp.where(ri == partner, 1.0, 0.0).astype(BF16)
    mean_blk = jnp.where((ri >> DH_SHIFT) == (ci >> DH_SHIFT), 1.0 / DA_DH, 0.0).astype(BF16)

    def prep(x_ref, w_ref, pw_ref, o_ref, scale):
        a = (w_ref[...] * scale) * cs
        bc = (pw_ref[...] * scale) * sn
        for hh in range(DA_HEADS):
            cols = slice(hh * LANES, (hh + 1) * LANES)
            xb = x_ref[:, cols]
            xf = xb.astype(F32)
            r = lax.rsqrt(_dot((xf * xf).astype(BF16), mean_blk) + EPS)
            px = _dot(xb, perm)
            o_ref[:, cols] = (r * (xf * a + px * bc)).astype(BF16)

    prep(q_ref, qw_ref, qpw_ref, qo_ref, DA_DH ** -0.5 * LOG2E)
    prep(k_ref, kw_ref, kpw_ref, ko_ref, 1.0)


def _qkprep(positions, p, invf, qw, qpw, kw, kpw):
    t = p.shape[0]
    tm = 512
    pos3 = positions.reshape(t // tm, 1, tm)
    vec = pl.BlockSpec((1, LANES), lambda i: (0, 0))
    return pl.pallas_call(
        _qkprep_kernel,
        out_shape=(jax.ShapeDtypeStruct((t, SEG_W), BF16), jax.ShapeDtypeStruct((t, SEG_W), BF16)),
        grid=(t // tm,),
        in_specs=[pl.BlockSpec((1, 1, tm), lambda i: (i, 0, 0)),
                  pl.BlockSpec((tm, SEG_W), lambda i: (i, SEG_DQ)),
                  pl.BlockSpec((tm, SEG_W), lambda i: (i, SEG_DK)),
                  pl.BlockSpec((ROPE_DIM // 2, LANES), lambda i: (0, 0)),
                  vec, vec, vec, vec],
        out_specs=(pl.BlockSpec((tm, SEG_W), lambda i: (i, 0)),
                   pl.BlockSpec((tm, SEG_W), lambda i: (i, 0))),
        compiler_params=_cparams(("parallel",)),
        name="qkprep",
    )(pos3, p, p, invf, qw, qpw, kw, kpw)


AT_GROUP = 4


def _attn_kernel(q_ref, k_ref, v_ref, lq1_ref, lk1_ref, lq2_ref, lk2_ref, sw_ref, o_ref, m_ref, acc_ref,
                 sa_ref, sb_ref, *, lambda_init, nq):
    T = AT_T
    lane = lax.broadcasted_iota(jnp.int32, (T, LANES), 1)
    sbufs = (sa_ref, sb_ref)
    stages = []
    for qi in range(nq):
        stages.append((qi, qi * T, T, True))
        stages += [(qi, 2 * w * T, 2 * T, False) for w in range(qi // 2)]
        if qi % 2:
            stages.append((qi, (qi - 1) * T, T, False))

    H = T // 2

    def chunk_mask(rows, cols, row0):
        r = lax.broadcasted_iota(jnp.int32, (rows, cols), 0) + row0
        cc = lax.broadcasted_iota(jnp.int32, (rows, cols), 1)
        return (cc >> CHUNK_SHIFT) <= (r >> CHUNK_SHIFT)

    def score(t):
        qi, k0, kw, diag = stages[t]
        q = q_ref[qi * T:(qi + 1) * T, :]
        kb = k_ref[k0:k0 + kw, :]
        zero = jnp.zeros_like(q)
        for mi, qm in enumerate((jnp.where(lane < DA_DH, q, zero), jnp.where(lane >= DA_DH, q, zero))):
            if diag:
                sbufs[t % 2][mi, :H, :H] = jnp.where(chunk_mask(H, H, 0), _dot_nt(qm[:H], kb[:H]), NEG)
                sbufs[t % 2][mi, H:, :T] = jnp.where(chunk_mask(H, T, H), _dot_nt(qm[H:], kb), NEG)
            else:
                sbufs[t % 2][mi, :, :kw] = _dot_nt(qm, kb)

    def softmax_pv(s, m_old, acc_old, vaug):
        rmax = jnp.max(s, axis=1, keepdims=True)
        ntile = s.shape[1] // LANES
        if m_old is None:
            m_new = jnp.broadcast_to(rmax, (s.shape[0], LANES))
            p = jnp.exp2(s - jnp.concatenate([m_new] * ntile, axis=1)).astype(BF16)
            return m_new, _dot(p, vaug)
        m_new = jnp.maximum(m_old, rmax)
        al = jnp.exp2(m_old - m_new)
        p = jnp.exp2(s - jnp.concatenate([m_new] * ntile, axis=1)).astype(BF16)
        return m_new, acc_old * jnp.concatenate([al, al], axis=1) + _dot(p, vaug)

    def consume(t):
        qi, k0, kw, diag = stages[t]
        vaug = jnp.concatenate([v_ref[k0:k0 + kw, :], jnp.ones((kw, LANES), BF16)], axis=1)
        for mi in range(2):
            if diag:
                m_ref[mi, :H], acc_ref[mi, :H] = softmax_pv(sbufs[t % 2][mi, :H, :H], None, None, vaug[:H])
                m_ref[mi, H:], acc_ref[mi, H:] = softmax_pv(sbufs[t % 2][mi, H:, :T], None, None, vaug)
            else:
                m_ref[mi], acc_ref[mi] = softmax_pv(sbufs[t % 2][mi, :, :kw], m_ref[mi], acc_ref[mi], vaug)
        if t + 1 == len(stages) or stages[t + 1][0] != qi:
            finalize(qi)

    def finalize(qi):
        acc0 = acc_ref[0]
        acc1 = acc_ref[1]
        o0 = acc0[:, :DA_DV] / acc0[:, DA_DV:]
        o1 = acc1[:, :DA_DV] / acc1[:, DA_DV:]
        lam = (jnp.exp(jnp.sum(lq1_ref[...] * lk1_ref[...], axis=-1, keepdims=True))
               - jnp.exp(jnp.sum(lq2_ref[...] * lk2_ref[...], axis=-1, keepdims=True)) + lambda_init)
        o = o0 - lam * o1
        ms = jnp.mean(o * o, axis=-1, keepdims=True)
        o_ref[qi * T:(qi + 1) * T, :] = (o * lax.rsqrt(ms + EPS) * (sw_ref[...] * (1.0 - lambda_init))).astype(BF16)

    always = (pl.program_id(0) >= 0, pl.program_id(1) >= 0)
    score(0)
    for g0 in range(0, len(stages), AT_GROUP):
        @pl.when(always[(g0 // AT_GROUP) % 2])
        def _():
            for t in range(g0, min(g0 + AT_GROUP, len(stages))):
                if t + 1 < len(stages):
                    score(t + 1)
                consume(t)


def _attn(qn, kn, p, lq1, lk1, lq2, lk2, sw, batch, seq, lambda_init):
    t = qn.shape[0]
    T = AT_T
    vcol0 = SEG_DV * SEG_W // LANES
    vec64 = pl.BlockSpec((1, DA_DH), lambda b, h: (0, 0))
    kern = functools.partial(_attn_kernel, lambda_init=lambda_init, nq=seq // T)
    return pl.pallas_call(
        kern,
        out_shape=jax.ShapeDtypeStruct((t, DA_WIDTH), BF16),
        grid=(batch, DA_HEADS),
        in_specs=[pl.BlockSpec((seq, LANES), lambda b, h: (b, h)),
                  pl.BlockSpec((seq, LANES), lambda b, h: (b, h)),
                  pl.BlockSpec((seq, LANES), lambda b, h: (b, vcol0 + h)),
                  vec64, vec64, vec64, vec64,
                  pl.BlockSpec((1, DA_DV), lambda b, h: (0, 0))],
        out_specs=pl.BlockSpec((seq, LANES), lambda b, h: (b, h)),
        scratch_shapes=[pltpu.VMEM((2, T, LANES), F32), pltpu.VMEM((2, T, 2 * LANES), F32),
                        pltpu.VMEM((2, T, 2 * T), F32), pltpu.VMEM((2, T, 2 * T), F32)],
        compiler_params=_cparams(("parallel", "parallel")),
        name="diffattn",
    )(qn, kn, p, lq1, lk1, lq2, lk2, sw)


def _outproj_kernel(x_ref, g_ref, a_ref, b_ref, wa_ref, wb_ref, o_ref):
    mix = _dot(a_ref[...], wa_ref[...]) + _dot(b_ref[...], wb_ref[...])
    o_ref[...] = x_ref[...] + g_ref[...] * mix


def _outproj(x2, mod3, hml, oda, w_bf, tpb):
    t = x2.shape[0]
    tm = 512
    tiles_per_batch = tpb // tm
    return pl.pallas_call(
        _outproj_kernel,
        out_shape=jax.ShapeDtypeStruct((t, D_MODEL), F32),
        grid=(t // tm,),
        in_specs=[pl.BlockSpec((tm, D_MODEL), lambda i: (i, 0)),
                  pl.BlockSpec((None, 1, D_MODEL), lambda i: ((i // tiles_per_batch) * 6 + 2, 0, 0)),
                  pl.BlockSpec((tm, ML_WIDTH), lambda i: (i, 0)),
                  pl.BlockSpec((tm, DA_WIDTH), lambda i: (i, 0)),
                  pl.BlockSpec((ML_WIDTH, D_MODEL), lambda i: (0, 0)),
                  pl.BlockSpec((DA_WIDTH, D_MODEL), lambda i: (1, 0))],
        out_specs=pl.BlockSpec((tm, D_MODEL), lambda i: (i, 0)),
        compiler_params=_cparams(("parallel",)),
        name="outproj",
    )(x2, mod3, hml, oda, w_bf, w_bf)


def _ffn_kernel(x_ref, nw_ref, sc_ref, sh_ref, gt_ref, wg_ref, wu_ref, wd_ref, o_ref, h_ref, *, tm, sub, nj):
    j = pl.program_id(1)

    @pl.when(j == 0)
    def _():
        def emit(off, h):
            h_ref[pl.ds(off, sub), :] = h.astype(BF16)
        _norm_mod_rows(x_ref, nw_ref, sc_ref, sh_ref, emit, tm, sub)

    def down():
        h = h_ref[...]
        g = _dot(h, wg_ref[...])
        u = _dot(h, wu_ref[...])
        a = (g * jax.nn.sigmoid(g) * u).astype(BF16)
        return _dot(a, wd_ref[...])

    @pl.when(j == 0)
    def _():
        o_ref[...] = down()

    @pl.when((j > 0) & (j < nj - 1))
    def _():
        o_ref[...] += down()

    @pl.when(j == nj - 1)
    def _():
        o_ref[...] = x_ref[...] + gt_ref[...] * (o_ref[...] + down())


def _ffn(x1, nw, mod3, wgu_bf, wd_bf, tpb):
    t = x1.shape[0]
    tm, tf, sub = 1024, 512, 128
    tiles_per_batch = tpb // tm
    nj = D_FF // tf
    kern = functools.partial(_ffn_kernel, tm=tm, sub=sub, nj=nj)
    mrow = lambda k: pl.BlockSpec((None, 1, D_MODEL), lambda i, j: ((i // tiles_per_batch) * 6 + k, 0, 0))
    return pl.pallas_call(
        kern,
        out_shape=jax.ShapeDtypeStruct((t, D_MODEL), F32),
        grid=(t // tm, nj),
        in_specs=[pl.BlockSpec((tm, D_MODEL), lambda i, j: (i, 0)),
                  pl.BlockSpec((1, D_MODEL), lambda i, j: (0, 0)),
                  mrow(4), mrow(3), mrow(5),
                  pl.BlockSpec((D_MODEL, tf), lambda i, j: (0, j)),
                  pl.BlockSpec((D_MODEL, tf), lambda i, j: (0, nj + j)),
                  pl.BlockSpec((tf, D_MODEL), lambda i, j: (j, 0))],
        out_specs=pl.BlockSpec((tm, D_MODEL), lambda i, j: (i, 0)),
        scratch_shapes=[pltpu.VMEM((tm, D_MODEL), BF16)],
        compiler_params=_cparams(("parallel", "arbitrary"), VMEM_LIMIT_BIG),
        name="ffn",
    )(x1, nw, mod3, mod3, mod3, wgu_bf, wgu_bf, wd_bf)


def _rope_partner_index():
    idx = np.arange(LANES)
    d = idx % DA_DH
    half = ROPE_DIM // 2
    return np.where(d < half, idx + half, np.where(d < ROPE_DIM, idx - half, idx))


def kernel(x, c, positions, norm1_w, norm2_w, w_ada, b_ada, w_in, mlstm_conv_w, mlstm_conv_b, mlstm_gate_b,
           mlstm_norm_w, q_norm_w, k_norm_w, lambda_q1, lambda_k1, lambda_q2, lambda_k2, subln_w, w_out,
           w_gate_up, w_down):
    B, S, D = x.shape
    T = B * S
    depth = w_in.shape[0]
    xf = x.reshape(T, D)
    c8 = jnp.pad(c, ((0, 8 - B), (0, 0)))

    inv_freq = ROPE_THETA ** (-jnp.arange(0, ROPE_DIM, 2, dtype=F32) / ROPE_DIM)
    invf = jnp.broadcast_to(inv_freq[:, None], (ROPE_DIM // 2, LANES))
    pidx = _rope_partner_index()

    for layer in range(depth):
        lambda_init = 0.8 - 0.6 * math.exp(-0.3 * layer)
        mod = _ada(c8, w_ada[layer], b_ada[layer].reshape(1, -1))
        mod3 = mod[:B].reshape(B * 6, 1, D)

        w_l = w_in[layer]
        w_t = w_l.T
        w_bf = _wprep(w_t)
        wgate = jnp.pad(w_t[GATE_LO:GATE_HI], ((0, LANES - 2 * ML_HEADS), (0, 0)))
        wg = wgate.astype(BF16)

        p, g, wgu_bf, wdn_bf, wo_bf = _inproj(xf, norm1_w[layer].reshape(1, D), mod3, w_bf, wg,
                                              w_gate_up[layer], w_down[layer], w_out[layer], S)

        gb = jnp.pad(mlstm_gate_b[layer].reshape(1, -1), ((0, 0), (0, LANES - 2 * ML_HEADS)))
        hml = _mlstm(p, g, mlstm_conv_w[layer], mlstm_conv_b[layer].reshape(1, -1), gb,
                     mlstm_norm_w[layer].reshape(1, ML_WIDTH), B, S)

        qw = jnp.tile(q_norm_w[layer], 2).reshape(1, LANES)
        kw = jnp.tile(k_norm_w[layer], 2).reshape(1, LANES)
        qn, kn = _qkprep(positions, p, invf, qw, qw[:, pidx], kw, kw[:, pidx])

        r64 = lambda a: a[layer].reshape(1, DA_DH).astype(F32)
        oda = _attn(qn, kn, p, r64(lambda_q1), r64(lambda_k1), r64(lambda_q2), r64(lambda_k2),
                    subln_w[layer].reshape(1, DA_DV), B, S, lambda_init)

        x1 = _outproj(xf, mod3, hml, oda, wo_bf, S)
        xf = _ffn(x1, norm2_w[layer].reshape(1, D), mod3, wgu_bf, wdn_bf, S)
    return xf.reshape(B, S, D)
```
